```python
import jax, jax.numpy as jnp
from jax import lax
import numpy as np

D_MODEL = 4096
BATCH = 4
SEQ = 4096
DEPTH = 4

N_MIXERS = 2
N_CONV_LAYERS = (DEPTH + 1) // 2
N_SGU_LAYERS = DEPTH // 2
CONV_WIDTH = 31
SGU_INNER = D_MODEL
SGU_HEADS = 8
SGU_HEAD_DIM = SGU_INNER // SGU_HEADS
CHUNK = 128
N_EXPERTS = 64
EXPERT_FF = 192
SHARED_FF = 192
TOP_K = 8
N_GROUPS = 8
TOPK_GROUPS = 4
ROUTED_SCALE = 2.5
PLE_DIM = 256
LN_EPS = 1e-5
DEEPNORM_ALPHA = (2.0 * DEPTH) ** 0.25
DEEPNORM_BETA = (8.0 * DEPTH) ** -0.25

kernel_name = 'hybrid_conv_sgu_moe_deepnorm_trunk'


def layer_norm(x, g, b):
    xf = x.astype(jnp.float32)
    mu = jnp.mean(xf, axis=-1, keepdims=True)
    xc = xf - mu
    var = jnp.mean(xc * xc, axis=-1, keepdims=True)
    y = xc * lax.rsqrt(var + LN_EPS) * g.astype(jnp.float32) + b.astype(jnp.float32)
    return y.astype(x.dtype)


def conv_mixer(x, w_in, b_in, dw, dw_b, ln_g, ln_b, w_out):
    h = x @ w_in + b_in
    a, g = jnp.split(h, 2, axis=-1)
    h = a * jax.nn.sigmoid(g)
    h = lax.conv_general_dilated(
        h, dw[:, None, :], window_strides=(1,), padding=[(CONV_WIDTH - 1, 0)],
        dimension_numbers=('NWC', 'WIO', 'NWC'), feature_group_count=D_MODEL) + dw_b
    h = jax.nn.silu(layer_norm(h, ln_g, ln_b))
    return h @ w_out


def sgu_mixer(x, w_in, b_in, lnv_g, lnv_b, w_s, b_s, w_out):
    bsz, seq = x.shape[0], x.shape[1]
    n_chunks = seq // CHUNK
    z = jax.nn.gelu(x @ w_in + b_in, approximate=False)
    u, v = jnp.split(z, 2, axis=-1)
    v = layer_norm(v, lnv_g, lnv_b)
    v = v.reshape(bsz, n_chunks, CHUNK, SGU_HEADS, SGU_HEAD_DIM)
    causal = jnp.tril(jnp.ones((CHUNK, CHUNK), dtype=bool))
    w_mask = jnp.where(causal[None], w_s, jnp.zeros_like(w_s))
    vs = jnp.einsum('hts,bcshd->bcthd', w_mask, v) + b_s.T[None, None, :, :, None]
    out = u * vs.reshape(bsz, seq, SGU_INNER)
    return out @ w_out


def moe(x, w_router, router_bias, w_gate, w_up, w_down, ws_gate, ws_up, ws_down):
    logits = jnp.einsum('bsd,ed->bse', x.astype(jnp.float32), w_router.astype(jnp.float32))
    s = jax.nn.sigmoid(logits)
    sel = s + router_bias.astype(jnp.float32)
    sel_g = sel.reshape(sel.shape[:-1] + (N_GROUPS, N_EXPERTS // N_GROUPS))
    group_score = jnp.sum(lax.top_k(sel_g, 2)[0], axis=-1)
    gidx = lax.top_k(group_score, TOPK_GROUPS)[1]
    group_mask = jnp.sum(jax.nn.one_hot(gidx, N_GROUPS, dtype=jnp.float32), axis=-2) > 0
    expert_mask = jnp.repeat(group_mask, N_EXPERTS // N_GROUPS, axis=-1)
    masked = jnp.where(expert_mask, sel, -jnp.inf)
    eidx = lax.top_k(masked, TOP_K)[1]
    w_sel = jnp.take_along_axis(s, eidx, axis=-1)
    w_sel = w_sel / jnp.sum(w_sel, axis=-1, keepdims=True) * ROUTED_SCALE
    combine = jnp.einsum('bske,bsk->bse',
                         jax.nn.one_hot(eidx, N_EXPERTS, dtype=jnp.float32), w_sel)
    combine = combine.astype(x.dtype)
    hg = jnp.einsum('bsd,edf->bsef', x, w_gate)
    hu = jnp.einsum('bsd,edf->bsef', x, w_up)
    act = jax.nn.silu(hg) * hu * combine[..., None]
    routed = jnp.einsum('bsef,efd->bsd', act, w_down)
    shared = (jax.nn.silu(x @ ws_gate) * (x @ ws_up)) @ ws_down
    return routed + shared


def per_layer_embed(x, p_i, w_pg_down, w_pg_up, w_pe):
    gate = jax.nn.sigmoid((x @ w_pg_down) @ w_pg_up)
    return gate * (p_i @ w_pe)


def setup_inputs(seed: int = 0) -> dict:
    key = jax.random.key(seed)
    ks = jax.random.split(key, 32)
    f32 = jnp.float32

    def nrm(k, shape, scale):
        return jax.random.normal(k, shape, dtype=f32) * scale

    beta = DEEPNORM_BETA
    inputs = {
        'x': nrm(ks[0], (BATCH, SEQ, D_MODEL), 1.0),
        'p': nrm(ks[1], (DEPTH, BATCH, SEQ, PLE_DIM), 1.0),
        'ln_g': 1.0 + nrm(ks[2], (DEPTH, 3, D_MODEL), 0.05),
        'ln_b': nrm(ks[3], (DEPTH, 3, D_MODEL), 0.02),
        'conv_w_in': nrm(ks[4], (N_CONV_LAYERS, D_MODEL, 2 * D_MODEL), D_MODEL ** -0.5),
        'conv_b_in': nrm(ks[5], (N_CONV_LAYERS, 2 * D_MODEL), 0.02),
        'conv_dw': nrm(ks[6], (N_CONV_LAYERS, CONV_WIDTH, D_MODEL), CONV_WIDTH ** -0.5),
        'conv_dw_b': nrm(ks[7], (N_CONV_LAYERS, D_MODEL), 0.02),
        'conv_ln_g': 1.0 + nrm(ks[8], (N_CONV_LAYERS, D_MODEL), 0.05),
        'conv_ln_b': nrm(ks[9], (N_CONV_LAYERS, D_MODEL), 0.02),
        'conv_w_out': nrm(ks[10], (N_CONV_LAYERS, D_MODEL, D_MODEL), beta * D_MODEL ** -0.5),
        'sgu_w_in': nrm(ks[11], (N_SGU_LAYERS, D_MODEL, 2 * SGU_INNER), D_MODEL ** -0.5),
        'sgu_b_in': nrm(ks[12], (N_SGU_LAYERS, 2 * SGU_INNER), 0.02),
        'sgu_lnv_g': 1.0 + nrm(ks[13], (N_SGU_LAYERS, SGU_INNER), 0.05),
        'sgu_lnv_b': nrm(ks[14], (N_SGU_LAYERS, SGU_INNER), 0.02),
        'sgu_w_s': nrm(ks[15], (N_SGU_LAYERS, SGU_HEADS, CHUNK, CHUNK), CHUNK ** -0.5),
        'sgu_b_s': 1.0 + nrm(ks[16], (N_SGU_LAYERS, SGU_HEADS, CHUNK), 0.1),
        'sgu_w_out': nrm(ks[17], (N_SGU_LAYERS, SGU_INNER, D_MODEL), beta * SGU_INNER ** -0.5),
        'moe_w_router': nrm(ks[18], (DEPTH, N_EXPERTS, D_MODEL), D_MODEL ** -0.5),
        'moe_router_bias': nrm(ks[19], (DEPTH, N_EXPERTS), 0.01),
        'moe_w_gate': nrm(ks[20], (DEPTH, N_EXPERTS, D_MODEL, EXPERT_FF), D_MODEL ** -0.5),
        'moe_w_up': nrm(ks[21], (DEPTH, N_EXPERTS, D_MODEL, EXPERT_FF), D_MODEL ** -0.5),
        'moe_w_down': nrm(ks[22], (DEPTH, N_EXPERTS, EXPERT_FF, D_MODEL), beta * EXPERT_FF ** -0.5),
        'moe_ws_gate': nrm(ks[23], (DEPTH, D_MODEL, SHARED_FF), D_MODEL ** -0.5),
        'moe_ws_up': nrm(ks[24], (DEPTH, D_MODEL, SHARED_FF), D_MODEL ** -0.5),
        'moe_ws_down': nrm(ks[25], (DEPTH, SHARED_FF, D_MODEL), beta * SHARED_FF ** -0.5),
        'ple_w_gate_down': nrm(ks[26], (DEPTH, D_MODEL, PLE_DIM), D_MODEL ** -0.5),
        'ple_w_gate_up': nrm(ks[27], (DEPTH, PLE_DIM, D_MODEL), PLE_DIM ** -0.5),
        'ple_w_proj': nrm(ks[28], (DEPTH, PLE_DIM, D_MODEL), beta * PLE_DIM ** -0.5),
    }
    return inputs


def reference(x, p, ln_g, ln_b,
              conv_w_in, conv_b_in, conv_dw, conv_dw_b, conv_ln_g, conv_ln_b, conv_w_out,
              sgu_w_in, sgu_b_in, sgu_lnv_g, sgu_lnv_b, sgu_w_s, sgu_b_s, sgu_w_out,
              moe_w_router, moe_router_bias, moe_w_gate, moe_w_up, moe_w_down,
              moe_ws_gate, moe_ws_up, moe_ws_down,
              ple_w_gate_down, ple_w_gate_up, ple_w_proj):
    alpha = DEEPNORM_ALPHA
    for i in range(DEPTH):
        j = i // N_MIXERS
        if i % N_MIXERS == 0:
            m = conv_mixer(x, conv_w_in[j], conv_b_in[j], conv_dw[j], conv_dw_b[j],
                           conv_ln_g[j], conv_ln_b[j], conv_w_out[j])
        else:
            m = sgu_mixer(x, sgu_w_in[j], sgu_b_in[j], sgu_lnv_g[j], sgu_lnv_b[j],
                          sgu_w_s[j], sgu_b_s[j], sgu_w_out[j])
        x = layer_norm(alpha * x + m, ln_g[i, 0], ln_b[i, 0])
        f = moe(x, moe_w_router[i], moe_router_bias[i], moe_w_gate[i], moe_w_up[i],
                moe_w_down[i], moe_ws_gate[i], moe_ws_up[i], moe_ws_down[i])
        x = layer_norm(alpha * x + f, ln_g[i, 1], ln_b[i, 1])
        e = per_layer_embed(x, p[i], ple_w_gate_down[i], ple_w_gate_up[i], ple_w_proj[i])
        x = layer_norm(alpha * x + e, ln_g[i, 2], ln_b[i, 2])
    return x
```

```python
import functools

import jax
import jax.numpy as jnp
from jax import lax
from jax.experimental import pallas as pl
from jax.experimental.pallas import tpu as pltpu

F32 = jnp.float32
BF16 = jnp.bfloat16
I32 = jnp.int32

TOP_K = 8
N_GROUPS = 8
TOPK_GROUPS = 4
ROUTED_SCALE = 2.5
LN_EPS = 1e-5

V7X_LANES = 128
V7X_SUBLANES = 8
V7X_MXU_DIM = 256
V7X_VMEM_BYTES = 64 * 1024 * 1024
VMEM_LIMIT = V7X_VMEM_BYTES - 8 * 1024 * 1024

CONV_HALO = 32


def _pick(n, pref, align):
    best = None
    for d in range(align, min(n, pref) + 1, align):
        if n % d == 0:
            best = d
    return n if best is None else best


def _cparams(sem):
    return pltpu.CompilerParams(dimension_semantics=sem, vmem_limit_bytes=VMEM_LIMIT)


def _sigmoid(x):
    return jax.nn.sigmoid(x)


def _layer_norm_chunks(chunks, gam_ref, bet_ref, tn):
    n_total = tn * len(chunks)
    s = chunks[0].sum(axis=-1, keepdims=True)
    for c in chunks[1:]:
        s = s + c.sum(axis=-1, keepdims=True)
    mu = s * (1.0 / n_total)
    cen = [c - mu for c in chunks]
    v = (cen[0] * cen[0]).sum(axis=-1, keepdims=True)
    for c in cen[1:]:
        v = v + (c * c).sum(axis=-1, keepdims=True)
    r = lax.rsqrt(v * (1.0 / n_total) + LN_EPS)
    out = []
    for j, c in enumerate(cen):
        g = gam_ref[:, j * tn:(j + 1) * tn]
        b = bet_ref[:, j * tn:(j + 1) * tn]
        out.append(c * r * g + b)
    return out


def _glu_kernel(x_ref, wa_ref, wg_ref, ba_ref, bg_ref, o_ref, xb_ref):
    @pl.when(pl.program_id(1) == 0)
    def _():
        xb_ref[...] = x_ref[...].astype(BF16)

    xb = xb_ref[...]
    a = jnp.dot(xb, wa_ref[...], preferred_element_type=F32) + ba_ref[...]
    g = jnp.dot(xb, wg_ref[...], preferred_element_type=F32) + bg_ref[...]
    o_ref[...] = a * _sigmoid(g)


def _glu_proj(x, w, b):
    n, k = x.shape
    m = w.shape[1] // 2
    tm = _pick(n, 512, V7X_SUBLANES)
    tn = _pick(m, 512, V7X_LANES)
    nj = m // tn
    return pl.pallas_call(
        _glu_kernel,
        grid=(n // tm, nj),
        in_specs=[
            pl.BlockSpec((tm, k), lambda i, j: (i, 0)),
            pl.BlockSpec((k, tn), lambda i, j: (0, j)),
            pl.BlockSpec((k, tn), lambda i, j: (0, j + nj)),
            pl.BlockSpec((1, tn), lambda i, j: (0, j)),
            pl.BlockSpec((1, tn), lambda i, j: (0, j + nj)),
        ],
        out_specs=pl.BlockSpec((tm, tn), lambda i, j: (i, j)),
        out_shape=jax.ShapeDtypeStruct((n, m), F32),
        scratch_shapes=[pltpu.VMEM((tm, k), BF16)],
        compiler_params=_cparams(("parallel", "arbitrary")),
        name="glu_proj",
    )(x, w, w, b, b)


def _gelu(x):
    return 0.5 * x * (1.0 + lax.erf(x * (0.5 ** 0.5)))


def _gelu_kernel(x_ref, w_ref, b_ref, o_ref, xb_ref):
    @pl.when(pl.program_id(1) == 0)
    def _():
        xb_ref[...] = x_ref[...].astype(BF16)

    o_ref[...] = _gelu(jnp.dot(xb_ref[...], w_ref[...], preferred_element_type=F32) + b_ref[...])


def _gelu_proj(x, w, b, m):
    n, k = x.shape
    tm = _pick(n, 512, V7X_SUBLANES)
    tn = _pick(m, 512, V7X_LANES)
    return pl.pallas_call(
        _gelu_kernel,
        grid=(n // tm, m // tn),
        in_specs=[
            pl.BlockSpec((tm, k), lambda i, j: (i, 0)),
            pl.BlockSpec((k, tn), lambda i, j: (0, j)),
            pl.BlockSpec((1, tn), lambda i, j: (0, j)),
        ],
        out_specs=pl.BlockSpec((tm, tn), lambda i, j: (i, j)),
        out_shape=jax.ShapeDtypeStruct((n, m), F32),
        scratch_shapes=[pltpu.VMEM((tm, k), BF16)],
        compiler_params=_cparams(("parallel", "arbitrary")),
        name="gelu_proj",
    )(x, w, b)


def _gelu_ln_kernel(x_ref, w_ref, b_ref, gam_ref, bet_ref, o_ref, xb_ref, acc_ref, *, nj, tn):
    j = pl.program_id(1)

    @pl.when(j == 0)
    def _():
        xb_ref[...] = x_ref[...].astype(BF16)

    acc_ref[j] = _gelu(jnp.dot(xb_ref[...], w_ref[...], preferred_element_type=F32) + b_ref[...])

    @pl.when(j == nj - 1)
    def _():
        outs = _layer_norm_chunks([acc_ref[c] for c in range(nj)], gam_ref, bet_ref, tn)
        for c in range(nj):
            o_ref[:, c * tn:(c + 1) * tn] = outs[c]


def _gelu_ln_proj(x, w, b, gam, bet, col0):
    n, k = x.shape
    m = gam.shape[1]
    tm = _pick(n, 256, V7X_SUBLANES)
    tn = _pick(m, 512, V7X_LANES)
    nj = m // tn
    j0 = col0 // tn
    return pl.pallas_call(
        functools.partial(_gelu_ln_kernel, nj=nj, tn=tn),
        grid=(n // tm, nj),
        in_specs=[
            pl.BlockSpec((tm, k), lambda i, j: (i, 0)),
            pl.BlockSpec((k, tn), lambda i, j: (0, j + j0)),
            pl.BlockSpec((1, tn), lambda i, j: (0, j + j0)),
            pl.BlockSpec((1, m), lambda i, j: (0, 0)),
            pl.BlockSpec((1, m), lambda i, j: (0, 0)),
        ],
        out_specs=pl.BlockSpec((tm, m), lambda i, j: (i, 0)),
        out_shape=jax.ShapeDtypeStruct((n, m), F32),
        scratch_shapes=[pltpu.VMEM((tm, k), BF16), pltpu.VMEM((nj, tm, tn), F32)],
        compiler_params=_cparams(("parallel", "arbitrary")),
        name="gelu_ln_proj",
    )(x, w, b, gam, bet)


def _res_ln_kernel(g_ref, w_ref, res_ref, gam_ref, bet_ref, o_ref, acc_ref, *, nj, tn, alpha):
    j = pl.program_id(1)
    acc_ref[j] = jnp.dot(g_ref[...], w_ref[...], preferred_element_type=F32)

    @pl.when(j == nj - 1)
    def _():
        pre = [alpha * res_ref[:, c * tn:(c + 1) * tn] + acc_ref[c] for c in range(nj)]
        outs = _layer_norm_chunks(pre, gam_ref, bet_ref, tn)
        for c in range(nj):
            o_ref[:, c * tn:(c + 1) * tn] = outs[c]


def _res_ln_proj(g, w, res, gam, bet, alpha):
    n, k = g.shape
    m = w.shape[1]
    tm = _pick(n, 256, 2 * V7X_SUBLANES)
    tn = _pick(m, 512, V7X_LANES)
    nj = m // tn
    return pl.pallas_call(
        functools.partial(_res_ln_kernel, nj=nj, tn=tn, alpha=alpha),
        grid=(n // tm, nj),
        in_specs=[
            pl.BlockSpec((tm, k), lambda i, j: (i, 0)),
            pl.BlockSpec((k, tn), lambda i, j: (0, j)),
            pl.BlockSpec((tm, m), lambda i, j: (i, 0)),
            pl.BlockSpec((1, m), lambda i, j: (0, 0)),
            pl.BlockSpec((1, m), lambda i, j: (0, 0)),
        ],
        out_specs=pl.BlockSpec((tm, m), lambda i, j: (i, 0)),
        out_shape=jax.ShapeDtypeStruct((n, m), F32),
        scratch_shapes=[pltpu.VMEM((nj, tm, tn), F32)],
        compiler_params=_cparams(("parallel", "arbitrary")),
        name="res_ln_proj",
    )(g, w, res, gam, bet)


def _conv_kernel(h_ref, dw_ref, dwb_ref, gam_ref, bet_ref, o_ref, ext_ref, cv_ref, *, width, ts, ncb, rc):
    s = pl.program_id(1)

    @pl.when(s == 0)
    def _():
        ext_ref[:, 0:CONV_HALO, :] = jnp.zeros((ncb, CONV_HALO, V7X_LANES), F32)

    @pl.when(s > 0)
    def _():
        ext_ref[:, 0:CONV_HALO, :] = ext_ref[:, ts:ts + CONV_HALO, :]

    for cb in range(ncb):
        ext_ref[cb, CONV_HALO:CONV_HALO + ts, :] = h_ref[:, cb * V7X_LANES:(cb + 1) * V7X_LANES]

    base = CONV_HALO - (width - 1)

    def col_body(cb, carry):
        def row_body(rb, carry2):
            r0 = pl.multiple_of(rb * rc, rc)
            acc = jnp.zeros((rc, V7X_LANES), F32)
            for k in range(width):
                acc = acc + ext_ref[cb, pl.ds(r0 + base + k, rc), :] * dw_ref[cb, k:k + 1, :]
            cv_ref[cb, pl.ds(r0, rc), :] = acc + dwb_ref[cb]
            return carry2

        return lax.fori_loop(0, ts // rc, row_body, carry)

    lax.fori_loop(0, ncb, col_body, 0)

    d_model = ncb * V7X_LANES
    ssum = cv_ref[0]
    for cb in range(1, ncb):
        ssum = ssum + cv_ref[cb]
    mu = ssum.sum(axis=-1, keepdims=True) * (1.0 / d_model)
    vsum = jnp.zeros((ts, V7X_LANES), F32)
    for cb in range(ncb):
        c = cv_ref[cb] - mu
        vsum = vsum + c * c
    r = lax.rsqrt(vsum.sum(axis=-1, keepdims=True) * (1.0 / d_model) + LN_EPS)
    for cb in range(ncb):
        y = (cv_ref[cb] - mu) * r * gam_ref[cb] + bet_ref[cb]
        o_ref[:, cb * V7X_LANES:(cb + 1) * V7X_LANES] = (y * _sigmoid(y)).astype(BF16)


def _conv_ln_silu(h, dw, dwb, gam, bet, batch, seq):
    n, d = h.shape
    width = dw.shape[0]
    assert width - 1 <= CONV_HALO
    ncb = d // V7X_LANES
    ts = _pick(seq, 256, CONV_HALO)
    nsb = seq // ts
    rc = _pick(ts, 128, V7X_SUBLANES)
    dw3 = jnp.pad(dw, ((0, CONV_HALO - width), (0, 0))).reshape(CONV_HALO, ncb, V7X_LANES).transpose(1, 0, 2)
    col3 = lambda v: v.reshape(ncb, 1, V7X_LANES)
    return pl.pallas_call(
        functools.partial(_conv_kernel, width=width, ts=ts, ncb=ncb, rc=rc),
        grid=(batch, nsb),
        in_specs=[
            pl.BlockSpec((ts, d), lambda b, s: (b * nsb + s, 0)),
            pl.BlockSpec((ncb, CONV_HALO, V7X_LANES), lambda b, s: (0, 0, 0)),
            pl.BlockSpec((ncb, 1, V7X_LANES), lambda b, s: (0, 0, 0)),
            pl.BlockSpec((ncb, 1, V7X_LANES), lambda b, s: (0, 0, 0)),
            pl.BlockSpec((ncb, 1, V7X_LANES), lambda b, s: (0, 0, 0)),
        ],
        out_specs=pl.BlockSpec((ts, d), lambda b, s: (b * nsb + s, 0)),
        out_shape=jax.ShapeDtypeStruct((n, d), BF16),
        scratch_shapes=[
            pltpu.VMEM((ncb, ts + CONV_HALO, V7X_LANES), F32),
            pltpu.VMEM((ncb, ts, V7X_LANES), F32),
        ],
        compiler_params=_cparams(("arbitrary", "arbitrary")),
        name="conv_ln_silu",
    )(h, dw3, col3(dwb), col3(gam), col3(bet))


def _sgu_kernel(u_ref, v_ref, ws_ref, bs_ref, o_ref, *, heads, chunk, hd, nck):
    row = lax.broadcasted_iota(I32, (chunk, chunk), 0)
    col = lax.broadcasted_iota(I32, (chunk, chunk), 1)
    causal = row >= col
    for h in range(heads):
        wm = jnp.where(causal, ws_ref[h], 0.0).astype(BF16)
        bias = bs_ref[h]
        for c in range(nck):
            rows = slice(c * chunk, (c + 1) * chunk)
            vb = v_ref[rows, h * hd:(h + 1) * hd].astype(BF16)
            vs = jnp.dot(wm, vb, preferred_element_type=F32)
            for q in range(hd // V7X_LANES):
                cols = slice(h * hd + q * V7X_LANES, h * hd + (q + 1) * V7X_LANES)
                gate = vs[:, q * V7X_LANES:(q + 1) * V7X_LANES] + bias
                o_ref[rows, cols] = (u_ref[rows, cols] * gate).astype(BF16)


def _sgu_gate(u, v, w_s, b_s):
    n, m = u.shape
    heads, chunk, _ = w_s.shape
    hd = m // heads
    assert hd % V7X_LANES == 0
    tc = _pick(n, 256, chunk)
    nck = tc // chunk
    bsb = jnp.broadcast_to(b_s[:, :, None], (heads, chunk, V7X_LANES))
    return pl.pallas_call(
        functools.partial(_sgu_kernel, heads=heads, chunk=chunk, hd=hd, nck=nck),
        grid=(n // tc,),
        in_specs=[
            pl.BlockSpec((tc, m), lambda i: (i, 0)),
            pl.BlockSpec((tc, m), lambda i: (i, 0)),
            pl.BlockSpec((heads, chunk, chunk), lambda i: (0, 0, 0)),
            pl.BlockSpec((heads, chunk, V7X_LANES), lambda i: (0, 0, 0)),
        ],
        out_specs=pl.BlockSpec((tc, m), lambda i: (i, 0)),
        out_shape=jax.ShapeDtypeStruct((n, m), BF16),
        compiler_params=_cparams(("parallel",)),
        name="sgu_gate",
    )(u, v, w_s, bsb)


def _router_kernel(x_ref, wrh_ref, wrl_ref, rb_ref, ltri_ref, wsgu_ref, wsd_ref,
                   mask_ref, slot_ref, wslot_ref, sh_ref, *, n_exp, fpad):
    x = x_ref[...]
    xh = x.astype(BF16)
    xl = (x - xh.astype(F32)).astype(BF16)
    tm = x.shape[0]

    lg = (jnp.dot(xh, wrh_ref[...], preferred_element_type=F32)
          + jnp.dot(xl, wrh_ref[...], preferred_element_type=F32)
          + jnp.dot(xh, wrl_ref[...], preferred_element_type=F32))
    lgt = lg.T[:n_exp]
    s = _sigmoid(lgt)
    sel = s + rb_ref[...][:n_exp, 0:1]

    gsz = n_exp // N_GROUPS
    neg = jnp.float32(-jnp.inf)
    sub = lax.broadcasted_iota(I32, (gsz, tm), 0)
    gscore = []
    for g in range(N_GROUPS):
        blk = sel[g * gsz:(g + 1) * gsz]
        m1 = blk.max(axis=0, keepdims=True)
        first = jnp.where(blk == m1, sub, gsz).min(axis=0, keepdims=True)
        m2 = jnp.where(sub == first, neg, blk).max(axis=0, keepdims=True)
        gscore.append(m1 + m2)
    masked = []
    for g in range(N_GROUPS):
        rank = jnp.zeros((1, tm), I32)
        for o in range(N_GROUPS):
            if o == g:
                continue
            beats = (gscore[o] > gscore[g]) | ((gscore[o] == gscore[g]) & (o < g))
            rank = rank + beats.astype(I32)
        keep = rank < TOPK_GROUPS
        masked.append(jnp.where(keep, sel[g * gsz:(g + 1) * gsz], neg))
    mk = jnp.concatenate(masked, axis=0)

    eidx = lax.broadcasted_iota(I32, (n_exp, tm), 0)
    rank = jnp.zeros((n_exp, tm), I32)
    for o in range(n_exp):
        row = mk[o:o + 1]
        beats = (row > mk) | ((row == mk) & (eidx > o))
        rank = rank + beats.astype(I32)
    chosen = rank < TOP_K
    w = jnp.where(chosen, s, 0.0)
    denom = w.sum(axis=0, keepdims=True)
    comb = w / denom * ROUTED_SCALE

    chosen_f = jnp.where(chosen, 1.0, 0.0)
    slot = jnp.dot(ltri_ref[...], chosen_f.astype(BF16), preferred_element_type=F32).astype(I32)
    mask_ref[...] = chosen.astype(I32)
    slot_ref[...] = slot
    for k in range(TOP_K):
        wslot_ref[k:k + 1, :] = jnp.where(chosen & (slot == k), comb, 0.0).sum(axis=0, keepdims=True)

    hs = jnp.dot(xh, wsgu_ref[...], preferred_element_type=F32)
    g = hs[:, :fpad]
    act = (g * _sigmoid(g)) * hs[:, fpad:]
    sh_ref[...] = jnp.dot(act.astype(BF16), wsd_ref[...], preferred_element_type=F32)


def _router_shared(x, wrh, wrl, rb, wsgu, wsd, n_exp):
    n, d = x.shape
    fpad = wsd.shape[0]
    tm = _pick(n, 256, V7X_LANES)
    ltri = (lax.broadcasted_iota(I32, (n_exp, n_exp), 0) > lax.broadcasted_iota(I32, (n_exp, n_exp), 1)).astype(BF16)
    full = lambda a: pl.BlockSpec(a.shape, lambda i: (0,) * a.ndim)
    return pl.pallas_call(
        functools.partial(_router_kernel, n_exp=n_exp, fpad=fpad),
        grid=(n // tm,),
        in_specs=[pl.BlockSpec((tm, d), lambda i: (i, 0)), full(wrh), full(wrl), full(rb), full(ltri),
                  full(wsgu), full(wsd)],
        out_specs=[
            pl.BlockSpec((n_exp, tm), lambda i: (0, i)),
            pl.BlockSpec((n_exp, tm), lambda i: (0, i)),
            pl.BlockSpec((TOP_K, tm), lambda i: (0, i)),
            pl.BlockSpec((tm, d), lambda i: (i, 0)),
        ],
        out_shape=[
            jax.ShapeDtypeStruct((n_exp, n), I32),
            jax.ShapeDtypeStruct((n_exp, n), I32),
            jax.ShapeDtypeStruct((TOP_K, n), F32),
            jax.ShapeDtypeStruct((n, d), F32),
        ],
        compiler_params=_cparams(("parallel",)),
        name="router_shared",
    )(x, wrh, wrl, rb, ltri, wsgu, wsd)


def _experts_kernel(te_ref, nt_ref, tokc_ref, tokn_ref, dst_ref, x_hbm, wgu_ref, wd_ref, y_hbm,
                    xbuf, ybuf, gsem, ssem, *, tmx, fpad, n_steps, n_real):
    i = pl.program_id(0)
    nt = nt_ref[0]
    slot = lax.rem(i, 2)

    def gather(tok_ref, sl):
        def body(r, c):
            pltpu.make_async_copy(x_hbm.at[pl.ds(tok_ref[0, 0, r], 1)], xbuf.at[sl, pl.ds(r, 1)],
                                  gsem.at[sl]).start()
            return c
        lax.fori_loop(0, tmx, body, 0)

    def wait_gather(sl):
        pltpu.make_async_copy(x_hbm.at[pl.ds(0, tmx)], xbuf.at[sl], gsem.at[sl]).wait()

    def scatter(sl):
        def body(r, c):
            pltpu.make_async_copy(ybuf.at[sl, pl.ds(r, 1)], y_hbm.at[pl.ds(dst_ref[0, 0, r], 1)],
                                  ssem.at[sl]).start()
            return c
        lax.fori_loop(0, tmx, body, 0)

    def wait_scatter(sl):
        pltpu.make_async_copy(ybuf.at[sl], y_hbm.at[pl.ds(0, tmx)], ssem.at[sl]).wait()

    @pl.when(i == 0)
    def _():
        ybuf[1] = jnp.zeros((tmx, ybuf.shape[2]), F32)
        fills = [pltpu.make_async_copy(ybuf.at[1], y_hbm.at[pl.ds(n_real + b * tmx, tmx)], ssem.at[1])
                 for b in range(2)]
        for f in fills:
            f.start()
        for f in fills:
            f.wait()

    @pl.when((i == 0) & (nt > 0))
    def _():
        gather(tokc_ref, 0)

    @pl.when(i + 1 < nt)
    def _():
        gather(tokn_ref, 1 - slot)

    @pl.when((i >= 2) & (i - 2 < nt))
    def _():
        wait_scatter(slot)

    @pl.when(i < nt)
    def _():
        wait_gather(slot)
        xb = xbuf[slot].astype(BF16)
        h = jnp.dot(xb, wgu_ref[0], preferred_element_type=F32)
        g = h[:, :fpad]
        act = (g * _sigmoid(g)) * h[:, fpad:]
        ybuf[slot] = jnp.dot(act.astype(BF16), wd_ref[0], preferred_element_type=F32)
        scatter(slot)

    @pl.when(i == n_steps - 1)
    def _():
        @pl.when((i >= 1) & (i - 1 < nt))
        def _():
            wait_scatter(1 - slot)

        @pl.when(i < nt)
        def _():
            wait_scatter(slot)


def _routed_experts(x, tile_expert, n_tiles, tok, dst, wgu, wd, n_rows_out, tmx):
    n, d = x.shape
    n_steps = tok.shape[0]
    fpad = wd.shape[1]
    grid_spec = pltpu.PrefetchScalarGridSpec(
        num_scalar_prefetch=2,
        grid=(n_steps,),
        in_specs=[
            pl.BlockSpec((1, 1, tmx), lambda i, te, nt: (i, 0, 0), memory_space=pltpu.SMEM),
            pl.BlockSpec((1, 1, tmx), lambda i, te, nt: (jnp.minimum(i + 1, n_steps - 1), 0, 0),
                         memory_space=pltpu.SMEM),
            pl.BlockSpec((1, 1, tmx), lambda i, te, nt: (i, 0, 0), memory_space=pltpu.SMEM),
            pl.BlockSpec(memory_space=pl.ANY),
            pl.BlockSpec((1, d, 2 * fpad), lambda i, te, nt: (te[i], 0, 0)),
            pl.BlockSpec((1, fpad, d), lambda i, te, nt: (te[i], 0, 0)),
        ],
        out_specs=pl.BlockSpec(memory_space=pl.ANY),
        scratch_shapes=[
            pltpu.VMEM((2, tmx, d), F32),
            pltpu.VMEM((2, tmx, d), F32),
            pltpu.SemaphoreType.DMA((2,)),
            pltpu.SemaphoreType.DMA((2,)),
        ],
    )
    return pl.pallas_call(
        functools.partial(_experts_kernel, tmx=tmx, fpad=fpad, n_steps=n_steps, n_real=n_rows_out - 2 * tmx),
        grid_spec=grid_spec,
        out_shape=jax.ShapeDtypeStruct((n_rows_out, d), F32),
        compiler_params=_cparams(("arbitrary",)),
        name="routed_experts",
    )(tile_expert, n_tiles, tok, tok, dst, x, wgu, wd)


def _dispatch_plan(mask, slot, n_exp, n, tmx, n_steps):
    pairs = n * TOP_K
    flat = jnp.nonzero(mask.reshape(-1), size=pairs, fill_value=0)[0].astype(I32)
    e_sorted = flat // n
    t_sorted = flat - e_sorted * n
    k_sorted = slot.reshape(-1)[flat]
    counts = mask.sum(axis=1).astype(I32)
    start = jnp.cumsum(counts) - counts
    tiles_per = (counts + tmx - 1) // tmx
    tile_end = jnp.cumsum(tiles_per)
    tile_start = tile_end - tiles_per
    n_tiles = tile_end[-1]
    tile_ids = jnp.arange(n_steps, dtype=I32)
    tile_e = jnp.minimum(jnp.searchsorted(tile_end, tile_ids, side="right").astype(I32), n_exp - 1)
    tile_e = jnp.where(tile_ids < n_tiles, tile_e, tile_e[jnp.maximum(n_tiles - 1, 0)])
    q = (tile_ids - tile_start[tile_e])[:, None] * tmx + jnp.arange(tmx, dtype=I32)[None, :]
    valid = (q < counts[tile_e][:, None]) & (tile_ids < n_tiles)[:, None]
    r = jnp.clip(start[tile_e][:, None] + q, 0, pairs - 1)
    tok = jnp.where(valid, t_sorted[r], 0)
    spare = pairs + (tile_ids % 2)[:, None] * tmx + jnp.arange(tmx, dtype=I32)[None, :]
    dst = jnp.where(valid, k_sorted[r] * n + t_sorted[r], spare)
    return tile_e, n_tiles.reshape(1).astype(I32), tok.reshape(n_steps, 1, tmx), dst.reshape(n_steps, 1, tmx)


def _combine_kernel(*refs, alpha):
    x_ref, sh_ref, w_ref = refs[0], refs[1], refs[2]
    y_refs = refs[3:3 + TOP_K]
    gam_ref, bet_ref, o_ref = refs[3 + TOP_K:]
    acc = alpha * x_ref[...] + sh_ref[...]
    w = w_ref[...]
    for k in range(TOP_K):
        acc = acc + w[:, k:k + 1] * y_refs[k][...]
    mu = acc.mean(axis=-1, keepdims=True)
    c = acc - mu
    var = (c * c).mean(axis=-1, keepdims=True)
    o_ref[...] = c * lax.rsqrt(var + LN_EPS) * gam_ref[...] + bet_ref[...]


def _moe_combine(x, shared, wtok, y, gam, bet, alpha):
    n, d = x.shape
    tm = _pick(n, 128, V7X_SUBLANES)
    nb = n // tm
    row = pl.BlockSpec((tm, d), lambda i: (i, 0))
    y_specs = [pl.BlockSpec((tm, d), functools.partial(lambda i, k: (k * nb + i, 0), k=k)) for k in range(TOP_K)]
    vec = pl.BlockSpec((1, d), lambda i: (0, 0))
    return pl.pallas_call(
        functools.partial(_combine_kernel, alpha=alpha),
        grid=(nb,),
        in_specs=[row, row, pl.BlockSpec((tm, TOP_K), lambda i: (i, 0))] + y_specs + [vec, vec],
        out_specs=row,
        out_shape=jax.ShapeDtypeStruct((n, d), F32),
        compiler_params=_cparams(("parallel",)),
        name="moe_combine",
    )(x, shared, wtok, *([y] * TOP_K), gam, bet)


def _ple_kernel(x_ref, p_ref, wdn_ref, wup_ref, wpe_ref, gam_ref, bet_ref, o_ref, *, alpha):
    x = x_ref[...]
    t = jnp.dot(x.astype(BF16), wdn_ref[...], preferred_element_type=F32)
    gate = _sigmoid(jnp.dot(t.astype(BF16), wup_ref[...], preferred_element_type=F32))
    pe = jnp.dot(p_ref[...].astype(BF16), wpe_ref[...], preferred_element_type=F32)
    acc = alpha * x + gate * pe
    mu = acc.mean(axis=-1, keepdims=True)
    c = acc - mu
    var = (c * c).mean(axis=-1, keepdims=True)
    o_ref[...] = c * lax.rsqrt(var + LN_EPS) * gam_ref[...] + bet_ref[...]


def _ple_ln(x, p, wdn, wup, wpe, gam, bet, alpha):
    n, d = x.shape
    pd = p.shape[1]
    tm = _pick(n, 256, V7X_SUBLANES)
    full = lambda a: pl.BlockSpec(a.shape, lambda i: (0,) * a.ndim)
    return pl.pallas_call(
        functools.partial(_ple_kernel, alpha=alpha),
        grid=(n // tm,),
        in_specs=[pl.BlockSpec((tm, d), lambda i: (i, 0)), pl.BlockSpec((tm, pd), lambda i: (i, 0)),
                  full(wdn), full(wup), full(wpe), full(gam), full(bet)],
        out_specs=pl.BlockSpec((tm, d), lambda i: (i, 0)),
        out_shape=jax.ShapeDtypeStruct((n, d), F32),
        compiler_params=_cparams(("parallel",)),
        name="ple_ln",
    )(x, p, wdn, wup, wpe, gam, bet)


def _pad_cols(w, to):
    return jnp.pad(w, [(0, 0)] * (w.ndim - 1) + [(0, to - w.shape[-1])])


def _moe_layer(x, w_router, router_bias, w_gate, w_up, w_down, ws_gate, ws_up, ws_down, gam, bet, alpha):
    n, d = x.shape
    n_exp, _, ff = w_gate.shape
    fpad = -(-ff // V7X_MXU_DIM) * V7X_MXU_DIM
    sfpad = -(-ws_gate.shape[1] // V7X_MXU_DIM) * V7X_MXU_DIM

    wr_t = _pad_cols(w_router.T, V7X_LANES)
    wrh = wr_t.astype(BF16)
    wrl = (wr_t - wrh.astype(F32)).astype(BF16)
    rb = jnp.broadcast_to(_pad_cols(router_bias[None, :], V7X_LANES).T, (V7X_LANES, V7X_LANES))
    wsgu = jnp.concatenate([_pad_cols(ws_gate, sfpad), _pad_cols(ws_up, sfpad)], axis=1).astype(BF16)
    wsd = jnp.pad(ws_down, ((0, sfpad - ws_down.shape[0]), (0, 0))).astype(BF16)
    wgu = jnp.concatenate([_pad_cols(w_gate, fpad), _pad_cols(w_up, fpad)], axis=2).astype(BF16)
    wd = jnp.pad(w_down, ((0, 0), (0, fpad - ff), (0, 0))).astype(BF16)

    mask, slot, wslot, shared = _router_shared(x, wrh, wrl, rb, wsgu, wsd, n_exp)

    tmx = _pick(n * TOP_K, 256, V7X_SUBLANES)
    n_steps = (n * TOP_K + n_exp * (tmx - 1)) // tmx + 1
    tile_e, n_tiles, tok, dst = _dispatch_plan(mask, slot, n_exp, n, tmx, n_steps)
    y = _routed_experts(x, tile_e, n_tiles, tok, dst, wgu, wd, n * TOP_K + 2 * tmx, tmx)
    return _moe_combine(x, shared, wslot.T, y, gam, bet, alpha)


def kernel(x, p, ln_g, ln_b, conv_w_in, conv_b_in, conv_dw, conv_dw_b, conv_ln_g, conv_ln_b, conv_w_out, sgu_w_in, sgu_b_in, sgu_lnv_g, sgu_lnv_b, sgu_w_s, sgu_b_s, sgu_w_out, moe_w_router, moe_router_bias, moe_w_gate, moe_w_up, moe_w_down, moe_ws_gate, moe_ws_up, moe_ws_down, ple_w_gate_down, ple_w_gate_up, ple_w_proj):
    batch, seq, d = x.shape
    depth = ln_g.shape[0]
    n = batch * seq
    alpha = (2.0 * depth) ** 0.25
    vec = lambda v: v.reshape(1, -1)

    xs = x.reshape(n, d)
    for i in range(depth):
        j = i // 2
        if i % 2 == 0:
            h = _glu_proj(xs, conv_w_in[j].astype(BF16), vec(conv_b_in[j]))
            g = _conv_ln_silu(h, conv_dw[j], conv_dw_b[j], conv_ln_g[j], conv_ln_b[j], batch, seq)
            w_out = conv_w_out[j]
        else:
            w_in = sgu_w_in[j].astype(BF16)
            inner = sgu_lnv_g.shape[1]
            u = _gelu_proj(xs, w_in, vec(sgu_b_in[j]), inner)
            v = _gelu_ln_proj(xs, w_in, vec(sgu_b_in[j]), vec(sgu_lnv_g[j]), vec(sgu_lnv_b[j]), inner)
            g = _sgu_gate(u, v, sgu_w_s[j], sgu_b_s[j])
            w_out = sgu_w_out[j]
        xs = _res_ln_proj(g, w_out.astype(BF16), xs, vec(ln_g[i, 0]), vec(ln_b[i, 0]), alpha)
        xs = _moe_layer(xs, moe_w_router[i], moe_router_bias[i], moe_w_gate[i], moe_w_up[i], moe_w_down[i],
                        moe_ws_gate[i], moe_ws_up[i], moe_ws_down[i], vec(ln_g[i, 1]), vec(ln_b[i, 1]), alpha)
        xs = _ple_ln(xs, p[i].reshape(n, -1), ple_w_gate_down[i].astype(BF16), ple_w_gate_up[i].astype(BF16),
                     ple_w_proj[i].astype(BF16), vec(ln_g[i, 2]), vec(ln_b[i, 2]), alpha)
    return xs.reshape(batch, seq, d)
```

```python
import functools

import jax
import jax.numpy as jnp
from jax import lax
from jax.experimental import pallas as pl
from jax.experimental.pallas import tpu as pltpu

F32 = jnp.float32
BF16 = jnp.bfloat16
I32 = jnp.int32

TOP_K = 8
N_GROUPS = 8
TOPK_GROUPS = 4
ROUTED_SCALE = 2.5
LN_EPS = 1e-5

V7X_LANES = 128
V7X_SUBLANES = 8
V7X_MXU_DIM = 256
V7X_VMEM_BYTES = 64 * 1024 * 1024
VMEM_LIMIT = V7X_VMEM_BYTES - 8 * 1024 * 1024

CONV_HALO = 32


def _pick(n, pref, align):
    best = None
    for d in range(align, min(n, pref) + 1, align):
        if n % d == 0:
            best = d
    return n if best is None else best


def _cparams(sem):
    return pltpu.CompilerParams(dimension_semantics=sem, vmem_limit_bytes=VMEM_LIMIT)


def _sigmoid(x):
    return jax.nn.sigmoid(x)


def _glu_kernel(x_ref, wa_ref, wg_ref, ba_ref, bg_ref, o_ref, xb_ref):
    @pl.when(pl.program_id(1) == 0)
    def _():
        xb_ref[...] = x_ref[...].astype(BF16)

    xb = xb_ref[...]
    a = jnp.dot(xb, wa_ref[...], preferred_element_type=F32) + ba_ref[...]
    g = jnp.dot(xb, wg_ref[...], preferred_element_type=F32) + bg_ref[...]
    o_ref[...] = a * _sigmoid(g)


def _glu_proj(x, w, b):
    n, k = x.shape
    m = w.shape[1] // 2
    tm = _pick(n, 512, V7X_SUBLANES)
    tn = _pick(m, 512, V7X_LANES)
    nj = m // tn
    return pl.pallas_call(
        _glu_kernel,
        grid=(n // tm, nj),
        in_specs=[
            pl.BlockSpec((tm, k), lambda i, j: (i, 0)),
            pl.BlockSpec((k, tn), lambda i, j: (0, j)),
            pl.BlockSpec((k, tn), lambda i, j: (0, j + nj)),
            pl.BlockSpec((1, tn), lambda i, j: (0, j)),
            pl.BlockSpec((1, tn), lambda i, j: (0, j + nj)),
        ],
        out_specs=pl.BlockSpec((tm, tn), lambda i, j: (i, j)),
        out_shape=jax.ShapeDtypeStruct((n, m), F32),
        scratch_shapes=[pltpu.VMEM((tm, k), BF16)],
        compiler_params=_cparams(("parallel", "arbitrary")),
        name="glu_proj",
    )(x, w, w, b, b)


def _gelu(x):
    return 0.5 * x * (1.0 + lax.erf(x * (0.5 ** 0.5)))


def _gelu_kernel(x_ref, w_ref, b_ref, o_ref, xb_ref):
    @pl.when(pl.program_id(1) == 0)
    def _():
        xb_ref[...] = x_ref[...].astype(BF16)

    o_ref[...] = _gelu(jnp.dot(xb_ref[...], w_ref[...], preferred_element_type=F32) + b_ref[...])


def _gelu_proj(x, w, b, m):
    n, k = x.shape
    tm = _pick(n, 512, V7X_SUBLANES)
    tn = _pick(m, 512, V7X_LANES)
    return pl.pallas_call(
        _gelu_kernel,
        grid=(n // tm, m // tn),
        in_specs=[
            pl.BlockSpec((tm, k), lambda i, j: (i, 0)),
            pl.BlockSpec((k, tn), lambda i, j: (0, j)),
            pl.BlockSpec((1, tn), lambda i, j: (0, j)),
        ],
        out_specs=pl.BlockSpec((tm, tn), lambda i, j: (i, j)),
        out_shape=jax.ShapeDtypeStruct((n, m), F32),
        scratch_shapes=[pltpu.VMEM((tm, k), BF16)],
        compiler_params=_cparams(("parallel", "arbitrary")),
        name="gelu_proj",
    )(x, w, b)


def _accumulate_matmul(lhs, w_ref, o_ref, k, tn):
    nch = o_ref.shape[1] // tn

    def chunk(c):
        return jnp.dot(lhs, w_ref[:, c * tn:(c + 1) * tn], preferred_element_type=F32)

    @pl.when(k == 0)
    def _():
        for c in range(nch):
            o_ref[:, c * tn:(c + 1) * tn] = chunk(c)

    @pl.when(k > 0)
    def _():
        for c in range(nch):
            o_ref[:, c * tn:(c + 1) * tn] += chunk(c)


def _ln_inplace(o_ref, pre_fn, gam_ref, bet_ref, tn):
    m = o_ref.shape[1]
    nch = m // tn
    s = None
    for c in range(nch):
        p = pre_fn(c)
        o_ref[:, c * tn:(c + 1) * tn] = p
        ps = p.sum(axis=-1, keepdims=True)
        s = ps if s is None else s + ps
    mu = s * (1.0 / m)
    v = None
    for c in range(nch):
        cen = o_ref[:, c * tn:(c + 1) * tn] - mu
        pv = (cen * cen).sum(axis=-1, keepdims=True)
        v = pv if v is None else v + pv
    r = lax.rsqrt(v * (1.0 / m) + LN_EPS)
    for c in range(nch):
        cols = slice(c * tn, (c + 1) * tn)
        o_ref[:, cols] = (o_ref[:, cols] - mu) * r * gam_ref[:, cols] + bet_ref[:, cols]


def _gelu_ln_kernel(x_ref, w_ref, b_ref, gam_ref, bet_ref, o_ref, *, nk, tn):
    k = pl.program_id(1)
    _accumulate_matmul(x_ref[...].astype(BF16), w_ref, o_ref, k, tn)

    @pl.when(k == nk - 1)
    def _():
        _ln_inplace(o_ref, lambda c: _gelu(o_ref[:, c * tn:(c + 1) * tn] + b_ref[:, c * tn:(c + 1) * tn]),
                    gam_ref, bet_ref, tn)


def _gelu_ln_proj(x, w, b, gam, bet, col0):
    n, kdim = x.shape
    m = gam.shape[1]
    assert col0 % m == 0
    tm = _pick(n, 512, V7X_SUBLANES)
    tk = _pick(kdim, 512, V7X_LANES)
    tn = _pick(m, 512, V7X_LANES)
    nk = kdim // tk
    jb = col0 // m
    return pl.pallas_call(
        functools.partial(_gelu_ln_kernel, nk=nk, tn=tn),
        grid=(n // tm, nk),
        in_specs=[
            pl.BlockSpec((tm, tk), lambda i, k: (i, k)),
            pl.BlockSpec((tk, m), lambda i, k: (k, jb)),
            pl.BlockSpec((1, m), lambda i, k: (0, jb)),
            pl.BlockSpec((1, m), lambda i, k: (0, 0)),
            pl.BlockSpec((1, m), lambda i, k: (0, 0)),
        ],
        out_specs=pl.BlockSpec((tm, m), lambda i, k: (i, 0)),
        out_shape=jax.ShapeDtypeStruct((n, m), F32),
        compiler_params=_cparams(("parallel", "arbitrary")),
        name="gelu_ln_proj",
    )(x, w, b, gam, bet)


def _res_ln_kernel(g_ref, w_ref, res_ref, gam_ref, bet_ref, o_ref, *, nk, tn, alpha):
    k = pl.program_id(1)
    _accumulate_matmul(g_ref[...], w_ref, o_ref, k, tn)

    @pl.when(k == nk - 1)
    def _():
        _ln_inplace(o_ref, lambda c: alpha * res_ref[:, c * tn:(c + 1) * tn] + o_ref[:, c * tn:(c + 1) * tn],
                    gam_ref, bet_ref, tn)


def _res_ln_proj(g, w, res, gam, bet, alpha):
    n, kdim = g.shape
    m = w.shape[1]
    tm = _pick(n, 512, 2 * V7X_SUBLANES)
    tk = _pick(kdim, 512, V7X_LANES)
    tn = _pick(m, 512, V7X_LANES)
    nk = kdim // tk
    return pl.pallas_call(
        functools.partial(_res_ln_kernel, nk=nk, tn=tn, alpha=alpha),
        grid=(n // tm, nk),
        in_specs=[
            pl.BlockSpec((tm, tk), lambda i, k: (i, k)),
            pl.BlockSpec((tk, m), lambda i, k: (k, 0)),
            pl.BlockSpec((tm, m), lambda i, k: (i, 0)),
            pl.BlockSpec((1, m), lambda i, k: (0, 0)),
            pl.BlockSpec((1, m), lambda i, k: (0, 0)),
        ],
        out_specs=pl.BlockSpec((tm, m), lambda i, k: (i, 0)),
        out_shape=jax.ShapeDtypeStruct((n, m), F32),
        compiler_params=_cparams(("parallel", "arbitrary")),
        name="res_ln_proj",
    )(g, w, res, gam, bet)


def _conv_kernel(h_ref, dw_ref, dwb_ref, gam_ref, bet_ref, o_ref, ext_ref, cv_ref, *, width, ts, ncb, rc):
    s = pl.program_id(1)

    @pl.when(s == 0)
    def _():
        ext_ref[:, 0:CONV_HALO, :] = jnp.zeros((ncb, CONV_HALO, V7X_LANES), F32)

    @pl.when(s > 0)
    def _():
        ext_ref[:, 0:CONV_HALO, :] = ext_ref[:, ts:ts + CONV_HALO, :]

    for cb in range(ncb):
        ext_ref[cb, CONV_HALO:CONV_HALO + ts, :] = h_ref[:, cb * V7X_LANES:(cb + 1) * V7X_LANES]

    base = CONV_HALO - (width - 1)

    def col_body(cb, carry):
        def row_body(rb, carry2):
            r0 = pl.multiple_of(rb * rc, rc)
            acc = jnp.zeros((rc, V7X_LANES), F32)
            for k in range(width):
                acc = acc + ext_ref[cb, pl.ds(r0 + base + k, rc), :] * dw_ref[cb, k:k + 1, :]
            cv_ref[cb, pl.ds(r0, rc), :] = acc + dwb_ref[cb]
            return carry2

        return lax.fori_loop(0, ts // rc, row_body, carry)

    lax.fori_loop(0, ncb, col_body, 0)

    d_model = ncb * V7X_LANES
    ssum = cv_ref[0]
    for cb in range(1, ncb):
        ssum = ssum + cv_ref[cb]
    mu = ssum.sum(axis=-1, keepdims=True) * (1.0 / d_model)
    vsum = jnp.zeros((ts, V7X_LANES), F32)
    for cb in range(ncb):
        c = cv_ref[cb] - mu
        vsum = vsum + c * c
    r = lax.rsqrt(vsum.sum(axis=-1, keepdims=True) * (1.0 / d_model) + LN_EPS)
    for cb in range(ncb):
        y = (cv_ref[cb] - mu) * r * gam_ref[cb] + bet_ref[cb]
        o_ref[:, cb * V7X_LANES:(cb + 1) * V7X_LANES] = (y * _sigmoid(y)).astype(BF16)


def _conv_ln_silu(h, dw, dwb, gam, bet, batch, seq):
    n, d = h.shape
    width = dw.shape[0]
    assert width - 1 <= CONV_HALO
    ncb = d // V7X_LANES
    ts = _pick(seq, 256, CONV_HALO)
    nsb = seq // ts
    rc = _pick(ts, 128, V7X_SUBLANES)
    dw3 = jnp.pad(dw, ((0, CONV_HALO - width), (0, 0))).reshape(CONV_HALO, ncb, V7X_LANES).transpose(1, 0, 2)
    col3 = lambda v: v.reshape(ncb, 1, V7X_LANES)
    return pl.pallas_call(
        functools.partial(_conv_kernel, width=width, ts=ts, ncb=ncb, rc=rc),
        grid=(batch, nsb),
        in_specs=[
            pl.BlockSpec((ts, d), lambda b, s: (b * nsb + s, 0)),
            pl.BlockSpec((ncb, CONV_HALO, V7X_LANES), lambda b, s: (0, 0, 0)),
            pl.BlockSpec((ncb, 1, V7X_LANES), lambda b, s: (0, 0, 0)),
            pl.BlockSpec((ncb, 1, V7X_LANES), lambda b, s: (0, 0, 0)),
            pl.BlockSpec((ncb, 1, V7X_LANES), lambda b, s: (0, 0, 0)),
        ],
        out_specs=pl.BlockSpec((ts, d), lambda b, s: (b * nsb + s, 0)),
        out_shape=jax.ShapeDtypeStruct((n, d), BF16),
        scratch_shapes=[
            pltpu.VMEM((ncb, ts + CONV_HALO, V7X_LANES), F32),
            pltpu.VMEM((ncb, ts, V7X_LANES), F32),
        ],
        compiler_params=_cparams(("arbitrary", "arbitrary")),
        name="conv_ln_silu",
    )(h, dw3, col3(dwb), col3(gam), col3(bet))


def _sgu_kernel(u_ref, v_ref, ws_ref, bs_ref, o_ref, *, heads, chunk, hd, nck):
    row = lax.broadcasted_iota(I32, (chunk, chunk), 0)
    col = lax.broadcasted_iota(I32, (chunk, chunk), 1)
    causal = row >= col
    for h in range(heads):
        wm = jnp.where(causal, ws_ref[h], 0.0).astype(BF16)
        bias = bs_ref[h]
        for c in range(nck):
            rows = slice(c * chunk, (c + 1) * chunk)
            vb = v_ref[rows, h * hd:(h + 1) * hd].astype(BF16)
            vs = jnp.dot(wm, vb, preferred_element_type=F32)
            for q in range(hd // V7X_LANES):
                cols = slice(h * hd + q * V7X_LANES, h * hd + (q + 1) * V7X_LANES)
                gate = vs[:, q * V7X_LANES:(q + 1) * V7X_LANES] + bias
                o_ref[rows, cols] = (u_ref[rows, cols] * gate).astype(BF16)


def _sgu_gate(u, v, w_s, b_s):
    n, m = u.shape
    heads, chunk, _ = w_s.shape
    hd = m // heads
    assert hd % V7X_LANES == 0
    tc = _pick(n, 256, chunk)
    nck = tc // chunk
    bsb = jnp.broadcast_to(b_s[:, :, None], (heads, chunk, V7X_LANES))
    return pl.pallas_call(
        functools.partial(_sgu_kernel, heads=heads, chunk=chunk, hd=hd, nck=nck),
        grid=(n // tc,),
        in_specs=[
            pl.BlockSpec((tc, m), lambda i: (i, 0)),
            pl.BlockSpec((tc, m), lambda i: (i, 0)),
            pl.BlockSpec((heads, chunk, chunk), lambda i: (0, 0, 0)),
            pl.BlockSpec((heads, chunk, V7X_LANES), lambda i: (0, 0, 0)),
        ],
        out_specs=pl.BlockSpec((tc, m), lambda i: (i, 0)),
        out_shape=jax.ShapeDtypeStruct((n, m), BF16),
        compiler_params=_cparams(("parallel",)),
        name="sgu_gate",
    )(u, v, w_s, bsb)


def _router_kernel(x_ref, wrh_ref, wrl_ref, rb_ref, ltri_ref, wsgu_ref, wsd_ref,
                   mask_ref, slot_ref, wslot_ref, sh_ref, *, n_exp, fpad):
    x = x_ref[...]
    xh = x.astype(BF16)
    xl = (x - xh.astype(F32)).astype(BF16)
    tm = x.shape[0]

    lg = (jnp.dot(xh, wrh_ref[...], preferred_element_type=F32)
          + jnp.dot(xl, wrh_ref[...], preferred_element_type=F32)
          + jnp.dot(xh, wrl_ref[...], preferred_element_type=F32))
    lgt = lg.T[:n_exp]
    s = _sigmoid(lgt)
    sel = s + rb_ref[...][:n_exp, 0:1]

    gsz = n_exp // N_GROUPS
    neg = jnp.float32(-jnp.inf)
    sub = lax.broadcasted_iota(I32, (gsz, tm), 0)
    gscore = []
    for g in range(N_GROUPS):
        blk = sel[g * gsz:(g + 1) * gsz]
        m1 = blk.max(axis=0, keepdims=True)
        first = jnp.where(blk == m1, sub, gsz).min(axis=0, keepdims=True)
        m2 = jnp.where(sub == first, neg, blk).max(axis=0, keepdims=True)
        gscore.append(m1 + m2)
    masked = []
    for g in range(N_GROUPS):
        rank = jnp.zeros((1, tm), I32)
        for o in range(N_GROUPS):
            if o == g:
                continue
            beats = (gscore[o] > gscore[g]) | ((gscore[o] == gscore[g]) & (o < g))
            rank = rank + beats.astype(I32)
        keep = rank < TOPK_GROUPS
        masked.append(jnp.where(keep, sel[g * gsz:(g + 1) * gsz], neg))
    mk = jnp.concatenate(masked, axis=0)

    eidx = lax.broadcasted_iota(I32, (n_exp, tm), 0)
    rank = jnp.zeros((n_exp, tm), I32)
    for o in range(n_exp):
        row = mk[o:o + 1]
        beats = (row > mk) | ((row == mk) & (eidx > o))
        rank = rank + beats.astype(I32)
    chosen = rank < TOP_K
    w = jnp.where(chosen, s, 0.0)
    denom = w.sum(axis=0, keepdims=True)
    comb = w / denom * ROUTED_SCALE

    chosen_f = jnp.where(chosen, 1.0, 0.0)
    slot = jnp.dot(ltri_ref[...], chosen_f.astype(BF16), preferred_element_type=F32).astype(I32)
    mask_ref[...] = chosen.astype(I32)
    slot_ref[...] = slot
    for k in range(TOP_K):
        wslot_ref[k:k + 1, :] = jnp.where(chosen & (slot == k), comb, 0.0).sum(axis=0, keepdims=True)

    hs = jnp.dot(xh, wsgu_ref[...], preferred_element_type=F32)
    g = hs[:, :fpad]
    act = (g * _sigmoid(g)) * hs[:, fpad:]
    sh_ref[...] = jnp.dot(act.astype(BF16), wsd_ref[...], preferred_element_type=F32)


def _router_shared(x, wrh, wrl, rb, wsgu, wsd, n_exp):
    n, d = x.shape
    fpad = wsd.shape[0]
    tm = _pick(n, 256, V7X_LANES)
    ltri = (lax.broadcasted_iota(I32, (n_exp, n_exp), 0) > lax.broadcasted_iota(I32, (n_exp, n_exp), 1)).astype(BF16)
    full = lambda a: pl.BlockSpec(a.shape, lambda i: (0,) * a.ndim)
    return pl.pallas_call(
        functools.partial(_router_kernel, n_exp=n_exp, fpad=fpad),
        grid=(n // tm,),
        in_specs=[pl.BlockSpec((tm, d), lambda i: (i, 0)), full(wrh), full(wrl), full(rb), full(ltri),
                  full(wsgu), full(wsd)],
        out_specs=[
            pl.BlockSpec((n_exp, tm), lambda i: (0, i)),
            pl.BlockSpec((n_exp, tm), lambda i: (0, i)),
            pl.BlockSpec((TOP_K, tm), lambda i: (0, i)),
            pl.BlockSpec((tm, d), lambda i: (i, 0)),
        ],
        out_shape=[
            jax.ShapeDtypeStruct((n_exp, n), I32),
            jax.ShapeDtypeStruct((n_exp, n), I32),
            jax.ShapeDtypeStruct((TOP_K, n), F32),
            jax.ShapeDtypeStruct((n, d), F32),
        ],
        compiler_params=_cparams(("parallel",)),
        name="router_shared",
    )(x, wrh, wrl, rb, ltri, wsgu, wsd)


def _experts_kernel(te_ref, tok0_ref, tokn_ref, dstp_ref, dstc_ref, x_hbm, wg_ref, wu_ref, wd_ref, y_hbm,
                    xbuf, ybuf, wgu_s, wd_s, gsem, ssem, *, tmx, ff, fpad, kc, n_steps, n_real):
    i = pl.program_id(0)
    d = xbuf.shape[2]

    def gather_row(tok_ref, sl, r, pri):
        pltpu.make_async_copy(x_hbm.at[pl.ds(tok_ref[0, 0, r], 1)], xbuf.at[sl, pl.ds(r, 1)],
                              gsem.at[sl]).start(priority=pri)

    def scatter_row(dst_ref, sl, r, pri):
        pltpu.make_async_copy(ybuf.at[sl, pl.ds(r, 1)], y_hbm.at[pl.ds(dst_ref[0, 0, r], 1)],
                              ssem.at[sl]).start(priority=pri)

    def wait_gather(sl):
        pltpu.make_async_copy(x_hbm.at[pl.ds(0, tmx)], xbuf.at[sl], gsem.at[sl]).wait()

    def wait_scatter(sl):
        pltpu.make_async_copy(ybuf.at[sl], y_hbm.at[pl.ds(0, tmx)], ssem.at[sl]).wait()

    def rows_loop(fn):
        def body(b, c):
            base = pl.multiple_of(b * V7X_SUBLANES, V7X_SUBLANES)
            for u in range(V7X_SUBLANES):
                fn(base + u, u % 2)
            return c
        lax.fori_loop(0, tmx // V7X_SUBLANES, body, 0)

    @pl.when(i == 0)
    def _():
        wgu_s[...] = jnp.zeros(wgu_s.shape, BF16)
        wd_s[...] = jnp.zeros(wd_s.shape, BF16)
        ybuf[1] = jnp.zeros((tmx, d), F32)
        fills = [pltpu.make_async_copy(ybuf.at[1], y_hbm.at[pl.ds(n_real + b * tmx, tmx)], ssem.at[1])
                 for b in range(2)]
        for f in fills:
            f.start()
        for f in fills:
            f.wait()
        rows_loop(lambda r, pri: gather_row(tok0_ref, 0, r, pri))

    @pl.when((i == 0) | (te_ref[i] != te_ref[jnp.maximum(i - 1, 0)]))
    def _():
        wgu_s[:, 0:ff] = wg_ref[0, 0].astype(BF16)
        wgu_s[:, fpad:fpad + ff] = wu_ref[0, 0].astype(BF16)
        wd_s[0:ff, :] = wd_ref[0, 0].astype(BF16)

    def step(slot):
        other = 1 - slot

        @pl.when(i > 0)
        def _():
            wait_scatter(slot)

        wait_gather(slot)
        nkc = d // kc
        per = tmx // nkc
        h = None
        for c in range(nkc):
            xk = xbuf[slot, :, c * kc:(c + 1) * kc].astype(BF16)
            part = jnp.dot(xk, wgu_s[c * kc:(c + 1) * kc, :], preferred_element_type=F32)
            h = part if h is None else h + part
            for r in range(c * per, (c + 1) * per):
                gather_row(tokn_ref, other, r, r % 2)
                scatter_row(dstp_ref, other, r, r % 2)
        g = h[:, :fpad]
        act = (g * _sigmoid(g)) * h[:, fpad:]
        ybuf[slot] = jnp.dot(act.astype(BF16), wd_s[...], preferred_element_type=F32)

        @pl.when(i == n_steps - 1)
        def _():
            rows_loop(lambda r, pri: scatter_row(dstc_ref, slot, r, pri))
            wait_scatter(other)
            wait_scatter(slot)
            wait_gather(other)

    for parity in range(2):
        pl.when(lax.rem(i, 2) == parity)(functools.partial(step, parity))


def _routed_experts(x, tile_expert, tok, dst_ext, w_gate, w_up, w_down, layer, n_rows_out, tmx):
    n, d = x.shape
    n_steps = tok.shape[0]
    ff = w_gate.shape[-1]
    fpad = -(-ff // V7X_MXU_DIM) * V7X_MXU_DIM
    kc = _pick(d, V7X_MXU_DIM, V7X_LANES)
    smem = functools.partial(pl.BlockSpec, (1, 1, tmx), memory_space=pltpu.SMEM)
    grid_spec = pltpu.PrefetchScalarGridSpec(
        num_scalar_prefetch=1,
        grid=(n_steps,),
        in_specs=[
            smem(lambda i, te: (0, 0, 0)),
            smem(lambda i, te: (jnp.minimum(i + 1, n_steps - 1), 0, 0)),
            smem(lambda i, te: (i, 0, 0)),
            smem(lambda i, te: (i + 1, 0, 0)),
            pl.BlockSpec(memory_space=pl.ANY),
            pl.BlockSpec((1, 1, d, ff), lambda i, te: (layer, te[i], 0, 0)),
            pl.BlockSpec((1, 1, d, ff), lambda i, te: (layer, te[i], 0, 0)),
            pl.BlockSpec((1, 1, ff, d), lambda i, te: (layer, te[i], 0, 0)),
        ],
        out_specs=pl.BlockSpec(memory_space=pl.ANY),
        scratch_shapes=[
            pltpu.VMEM((2, tmx, d), F32),
            pltpu.VMEM((2, tmx, d), F32),
            pltpu.VMEM((d, 2 * fpad), BF16),
            pltpu.VMEM((fpad, d), BF16),
            pltpu.SemaphoreType.DMA((2,)),
            pltpu.SemaphoreType.DMA((2,)),
        ],
    )
    return pl.pallas_call(
        functools.partial(_experts_kernel, tmx=tmx, ff=ff, fpad=fpad, kc=kc, n_steps=n_steps,
                          n_real=n_rows_out - 2 * tmx),
        grid_spec=grid_spec,
        out_shape=jax.ShapeDtypeStruct((n_rows_out, d), F32),
        compiler_params=_cparams(("arbitrary",)),
        name="routed_experts",
    )(tile_expert, tok, tok, dst_ext, dst_ext, x, w_gate, w_up, w_down)


def _dispatch_plan(mask, slot, n, tmx, n_steps):
    pairs = n * TOP_K
    flat = jnp.nonzero(mask.reshape(-1), size=pairs, fill_value=0)[0].astype(I32)
    e_sorted = flat // n
    t_sorted = flat - e_sorted * n
    k_sorted = slot.reshape(-1)[flat]
    counts = mask.sum(axis=1).astype(I32)
    start = jnp.cumsum(counts) - counts
    tiles_per = (counts + tmx - 1) // tmx
    tile_end = jnp.cumsum(tiles_per)
    tile_start = tile_end - tiles_per
    n_tiles = tile_end[-1]
    tile_ids = jnp.minimum(jnp.arange(n_steps, dtype=I32), n_tiles - 1)
    tile_e = (tile_ids[:, None] >= tile_end[None, :]).sum(axis=1).astype(I32)
    real = jnp.arange(n_steps, dtype=I32) < n_tiles
    lane = jnp.arange(tmx, dtype=I32)[None, :]
    q = (tile_ids - tile_start[tile_e])[:, None] * tmx + lane
    valid = (q < counts[tile_e][:, None]) & real[:, None]
    r = jnp.clip(start[tile_e][:, None] + q, 0, pairs - 1)
    tok = jnp.where(valid, t_sorted[r], 0)
    bank = (jnp.arange(-1, n_steps, dtype=I32) % 2)[:, None]
    spare = pairs + bank * tmx + lane
    dst = jnp.where(valid, k_sorted[r] * n + t_sorted[r], spare[1:])
    dst_ext = jnp.concatenate([spare[:1], dst], axis=0)
    return tile_e, tok.reshape(n_steps, 1, tmx), dst_ext.reshape(n_steps + 1, 1, tmx)


def _combine_kernel(*refs, alpha):
    x_ref, sh_ref, w_ref = refs[0], refs[1], refs[2]
    y_refs = refs[3:3 + TOP_K]
    gam_ref, bet_ref, o_ref = refs[3 + TOP_K:]
    acc = alpha * x_ref[...] + sh_ref[...]
    w = w_ref[...]
    for k in range(TOP_K):
        acc = acc + w[:, k:k + 1] * y_refs[k][...]
    mu = acc.mean(axis=-1, keepdims=True)
    c = acc - mu
    var = (c * c).mean(axis=-1, keepdims=True)
    o_ref[...] = c * lax.rsqrt(var + LN_EPS) * gam_ref[...] + bet_ref[...]


def _moe_combine(x, shared, wtok, y, gam, bet, alpha):
    n, d = x.shape
    tm = _pick(n, 128, V7X_SUBLANES)
    nb = n // tm
    row = pl.BlockSpec((tm, d), lambda i: (i, 0))
    y_specs = [pl.BlockSpec((tm, d), functools.partial(lambda i, k: (k * nb + i, 0), k=k)) for k in range(TOP_K)]
    vec = pl.BlockSpec((1, d), lambda i: (0, 0))
    return pl.pallas_call(
        functools.partial(_combine_kernel, alpha=alpha),
        grid=(nb,),
        in_specs=[row, row, pl.BlockSpec((tm, TOP_K), lambda i: (i, 0))] + y_specs + [vec, vec],
        out_specs=row,
        out_shape=jax.ShapeDtypeStruct((n, d), F32),
        compiler_params=_cparams(("parallel",)),
        name="moe_combine",
    )(x, shared, wtok, *([y] * TOP_K), gam, bet)


def _ple_kernel(x_ref, p_ref, wdn_ref, wup_ref, wpe_ref, gam_ref, bet_ref, o_ref, *, alpha):
    x = x_ref[...]
    t = jnp.dot(x.astype(BF16), wdn_ref[...], preferred_element_type=F32)
    gate = _sigmoid(jnp.dot(t.astype(BF16), wup_ref[...], preferred_element_type=F32))
    pe = jnp.dot(p_ref[...].astype(BF16), wpe_ref[...], preferred_element_type=F32)
    acc = alpha * x + gate * pe
    mu = acc.mean(axis=-1, keepdims=True)
    c = acc - mu
    var = (c * c).mean(axis=-1, keepdims=True)
    o_ref[...] = c * lax.rsqrt(var + LN_EPS) * gam_ref[...] + bet_ref[...]


def _ple_ln(x, p, wdn, wup, wpe, gam, bet, alpha):
    n, d = x.shape
    pd = p.shape[1]
    tm = _pick(n, 256, V7X_SUBLANES)
    full = lambda a: pl.BlockSpec(a.shape, lambda i: (0,) * a.ndim)
    return pl.pallas_call(
        functools.partial(_ple_kernel, alpha=alpha),
        grid=(n // tm,),
        in_specs=[pl.BlockSpec((tm, d), lambda i: (i, 0)), pl.BlockSpec((tm, pd), lambda i: (i, 0)),
                  full(wdn), full(wup), full(wpe), full(gam), full(bet)],
        out_specs=pl.BlockSpec((tm, d), lambda i: (i, 0)),
        out_shape=jax.ShapeDtypeStruct((n, d), F32),
        compiler_params=_cparams(("parallel",)),
        name="ple_ln",
    )(x, p, wdn, wup, wpe, gam, bet)


def _pad_cols(w, to):
    return jnp.pad(w, [(0, 0)] * (w.ndim - 1) + [(0, to - w.shape[-1])])


def _moe_layer(x, layer, w_router, router_bias, w_gate, w_up, w_down, ws_gate, ws_up, ws_down, gam, bet, alpha):
    n, d = x.shape
    n_exp = w_gate.shape[1]
    sfpad = -(-ws_gate.shape[1] // V7X_MXU_DIM) * V7X_MXU_DIM

    wr_t = _pad_cols(w_router.T, V7X_LANES)
    wrh = wr_t.astype(BF16)
    wrl = (wr_t - wrh.astype(F32)).astype(BF16)
    rb = jnp.broadcast_to(_pad_cols(router_bias[None, :], V7X_LANES).T, (V7X_LANES, V7X_LANES))
    wsgu = jnp.concatenate([_pad_cols(ws_gate, sfpad), _pad_cols(ws_up, sfpad)], axis=1).astype(BF16)
    wsd = jnp.pad(ws_down, ((0, sfpad - ws_down.shape[0]), (0, 0))).astype(BF16)

    mask, slot, wslot, shared = _router_shared(x, wrh, wrl, rb, wsgu, wsd, n_exp)

    tmx = _pick(n * TOP_K, 256, V7X_SUBLANES)
    n_steps = (n * TOP_K + n_exp * (tmx - 1)) // tmx + 1
    tile_e, tok, dst_ext = _dispatch_plan(mask, slot, n, tmx, n_steps)
    y = _routed_experts(x, tile_e, tok, dst_ext, w_gate, w_up, w_down, layer, n * TOP_K + 2 * tmx, tmx)
    return _moe_combine(x, shared, wslot.T, y, gam, bet, alpha)


def kernel(x, p, ln_g, ln_b, conv_w_in, conv_b_in, conv_dw, conv_dw_b, conv_ln_g, conv_ln_b, conv_w_out, sgu_w_in, sgu_b_in, sgu_lnv_g, sgu_lnv_b, sgu_w_s, sgu_b_s, sgu_w_out, moe_w_router, moe_router_bias, moe_w_gate, moe_w_up, moe_w_down, moe_ws_gate, moe_ws_up, moe_ws_down, ple_w_gate_down, ple_w_gate_up, ple_w_proj):
    batch, seq, d = x.shape
    depth = ln_g.shape[0]
    n = batch * seq
    alpha = (2.0 * depth) ** 0.25
    vec = lambda v: v.reshape(1, -1)

    xs = x.reshape(n, d)
    for i in range(depth):
        j = i // 2
        if i % 2 == 0:
            h = _glu_proj(xs, conv_w_in[j].astype(BF16), vec(conv_b_in[j]))
            g = _conv_ln_silu(h, conv_dw[j], conv_dw_b[j], conv_ln_g[j], conv_ln_b[j], batch, seq)
            w_out = conv_w_out[j]
        else:
            w_in = sgu_w_in[j].astype(BF16)
            inner = sgu_lnv_g.shape[1]
            u = _gelu_proj(xs, w_in, vec(sgu_b_in[j]), inner)
            v = _gelu_ln_proj(xs, w_in, vec(sgu_b_in[j]), vec(sgu_lnv_g[j]), vec(sgu_lnv_b[j]), inner)
            g = _sgu_gate(u, v, sgu_w_s[j], sgu_b_s[j])
            w_out = sgu_w_out[j]
        xs = _res_ln_proj(g, w_out.astype(BF16), xs, vec(ln_g[i, 0]), vec(ln_b[i, 0]), alpha)
        xs = _moe_layer(xs, i, moe_w_router[i], moe_router_bias[i], moe_w_gate, moe_w_up, moe_w_down,
                        moe_ws_gate[i], moe_ws_up[i], moe_ws_down[i], vec(ln_g[i, 1]), vec(ln_b[i, 1]), alpha)
        xs = _ple_ln(xs, p[i].reshape(n, -1), ple_w_gate_down[i].astype(BF16), ple_w_gate_up[i].astype(BF16),
                     ple_w_proj[i].astype(BF16), vec(ln_g[i, 2]), vec(ln_b[i, 2]), alpha)
    return xs.reshape(batch, seq, d)
```

```python
import functools

import jax
import jax.numpy as jnp
from jax import lax
from jax.experimental import pallas as pl
from jax.experimental.pallas import tpu as pltpu

F32 = jnp.float32
BF16 = jnp.bfloat16
I32 = jnp.int32
U32 = jnp.uint32

TOP_K = 8
N_GROUPS = 8
TOPK_GROUPS = 4
ROUTED_SCALE = 2.5
LN_EPS = 1e-5

V7X_LANES = 128
V7X_SUBLANES = 8
V7X_MXU_DIM = 256
V7X_VMEM_BYTES = 64 * 1024 * 1024
VMEM_LIMIT = V7X_VMEM_BYTES - 4 * 1024 * 1024

CONV_HALO = 32


def _pick(n, pref, align):
    best = None
    for d in range(align, min(n, pref) + 1, align):
        if n % d == 0:
            best = d
    return n if best is None else best


def _cparams(sem):
    return pltpu.CompilerParams(dimension_semantics=sem, vmem_limit_bytes=VMEM_LIMIT)


def _sigmoid(x):
    return jax.nn.sigmoid(x)


def _pack_bf16_pair(lo, hi):
    lo_bits = lax.bitcast_convert_type(lo.astype(BF16).astype(F32), U32)
    hi_bits = lax.bitcast_convert_type(hi.astype(BF16).astype(F32), U32)
    return (lo_bits >> 16) | (hi_bits & jnp.uint32(0xFFFF0000))


def _unpack_bf16_pair(p):
    lo = lax.bitcast_convert_type(p << 16, F32)
    hi = lax.bitcast_convert_type(p & jnp.uint32(0xFFFF0000), F32)
    return lo, hi


def _glu_kernel(x_ref, wa_ref, wg_ref, ba_ref, bg_ref, o_ref, xb_ref):
    @pl.when(pl.program_id(1) == 0)
    def _():
        xb_ref[...] = x_ref[...].astype(BF16)

    xb = xb_ref[...]
    a = jnp.dot(xb, wa_ref[...], preferred_element_type=F32) + ba_ref[...]
    g = jnp.dot(xb, wg_ref[...], preferred_element_type=F32) + bg_ref[...]
    o_ref[...] = a * _sigmoid(g)


def _glu_proj(x, w, b):
    n, k = x.shape
    m = w.shape[1] // 2
    tm = _pick(n, 512, V7X_SUBLANES)
    tn = _pick(m, 512, V7X_LANES)
    nj = m // tn
    return pl.pallas_call(
        _glu_kernel,
        grid=(n // tm, nj),
        in_specs=[
            pl.BlockSpec((tm, k), lambda i, j: (i, 0)),
            pl.BlockSpec((k, tn), lambda i, j: (0, j)),
            pl.BlockSpec((k, tn), lambda i, j: (0, j + nj)),
            pl.BlockSpec((1, tn), lambda i, j: (0, j)),
            pl.BlockSpec((1, tn), lambda i, j: (0, j + nj)),
        ],
        out_specs=pl.BlockSpec((tm, tn), lambda i, j: (i, j)),
        out_shape=jax.ShapeDtypeStruct((n, m), F32),
        scratch_shapes=[pltpu.VMEM((tm, k), BF16)],
        compiler_params=_cparams(("parallel", "arbitrary")),
        name="glu_proj",
    )(x, w, w, b, b)


def _gelu(x):
    return 0.5 * x * (1.0 + lax.erf(x * (0.5 ** 0.5)))


def _gelu_kernel(x_ref, w_ref, b_ref, o_ref, xb_ref):
    @pl.when(pl.program_id(1) == 0)
    def _():
        xb_ref[...] = x_ref[...].astype(BF16)

    o_ref[...] = _gelu(jnp.dot(xb_ref[...], w_ref[...], preferred_element_type=F32) + b_ref[...])


def _gelu_proj(x, w, b, m):
    n, k = x.shape
    tm = _pick(n, 512, V7X_SUBLANES)
    tn = _pick(m, 512, V7X_LANES)
    return pl.pallas_call(
        _gelu_kernel,
        grid=(n // tm, m // tn),
        in_specs=[
            pl.BlockSpec((tm, k), lambda i, j: (i, 0)),
            pl.BlockSpec((k, tn), lambda i, j: (0, j)),
            pl.BlockSpec((1, tn), lambda i, j: (0, j)),
        ],
        out_specs=pl.BlockSpec((tm, tn), lambda i, j: (i, j)),
        out_shape=jax.ShapeDtypeStruct((n, m), F32),
        scratch_shapes=[pltpu.VMEM((tm, k), BF16)],
        compiler_params=_cparams(("parallel", "arbitrary")),
        name="gelu_proj",
    )(x, w, b)


def _accumulate_matmul(lhs, w_ref, o_ref, k, tn):
    nch = o_ref.shape[1] // tn

    def chunk(c):
        return jnp.dot(lhs, w_ref[:, c * tn:(c + 1) * tn], preferred_element_type=F32)

    @pl.when(k == 0)
    def _():
        for c in range(nch):
            o_ref[:, c * tn:(c + 1) * tn] = chunk(c)

    @pl.when(k > 0)
    def _():
        for c in range(nch):
            o_ref[:, c * tn:(c + 1) * tn] += chunk(c)


def _ln_inplace(o_ref, pre_fn, gam_ref, bet_ref, tn):
    m = o_ref.shape[1]
    nch = m // tn
    s = None
    for c in range(nch):
        p = pre_fn(c)
        o_ref[:, c * tn:(c + 1) * tn] = p
        ps = p.sum(axis=-1, keepdims=True)
        s = ps if s is None else s + ps
    mu = s * (1.0 / m)
    v = None
    for c in range(nch):
        cen = o_ref[:, c * tn:(c + 1) * tn] - mu
        pv = (cen * cen).sum(axis=-1, keepdims=True)
        v = pv if v is None else v + pv
    r = lax.rsqrt(v * (1.0 / m) + LN_EPS)
    for c in range(nch):
        cols = slice(c * tn, (c + 1) * tn)
        o_ref[:, cols] = (o_ref[:, cols] - mu) * r * gam_ref[:, cols] + bet_ref[:, cols]


def _gelu_ln_kernel(x_ref, w_ref, b_ref, gam_ref, bet_ref, o_ref, *, nk, tn):
    k = pl.program_id(1)
    _accumulate_matmul(x_ref[...].astype(BF16), w_ref, o_ref, k, tn)

    @pl.when(k == nk - 1)
    def _():
        _ln_inplace(o_ref, lambda c: _gelu(o_ref[:, c * tn:(c + 1) * tn] + b_ref[:, c * tn:(c + 1) * tn]),
                    gam_ref, bet_ref, tn)


def _gelu_ln_proj(x, w, b, gam, bet, col0):
    n, kdim = x.shape
    m = gam.shape[1]
    assert col0 % m == 0
    tm = _pick(n, 512, V7X_SUBLANES)
    tk = _pick(kdim, 512, V7X_LANES)
    tn = _pick(m, 512, V7X_LANES)
    nk = kdim // tk
    jb = col0 // m
    return pl.pallas_call(
        functools.partial(_gelu_ln_kernel, nk=nk, tn=tn),
        grid=(n // tm, nk),
        in_specs=[
            pl.BlockSpec((tm, tk), lambda i, k: (i, k)),
            pl.BlockSpec((tk, m), lambda i, k: (k, jb)),
            pl.BlockSpec((1, m), lambda i, k: (0, jb)),
            pl.BlockSpec((1, m), lambda i, k: (0, 0)),
            pl.BlockSpec((1, m), lambda i, k: (0, 0)),
        ],
        out_specs=pl.BlockSpec((tm, m), lambda i, k: (i, 0)),
        out_shape=jax.ShapeDtypeStruct((n, m), F32),
        compiler_params=_cparams(("parallel", "arbitrary")),
        name="gelu_ln_proj",
    )(x, w, b, gam, bet)


def _res_ln_kernel(g_ref, w_ref, res_ref, gam_ref, bet_ref, o_ref, op_ref, *, nk, tn, alpha):
    k = pl.program_id(1)
    _accumulate_matmul(g_ref[...], w_ref, o_ref, k, tn)

    @pl.when(k == nk - 1)
    def _():
        _ln_inplace(o_ref, lambda c: alpha * res_ref[:, c * tn:(c + 1) * tn] + o_ref[:, c * tn:(c + 1) * tn],
                    gam_ref, bet_ref, tn)
        half = o_ref.shape[1] // 2
        for c in range(half // tn):
            cols = slice(c * tn, (c + 1) * tn)
            op_ref[:, cols] = _pack_bf16_pair(o_ref[:, cols], o_ref[:, half + c * tn:half + (c + 1) * tn])


def _res_ln_proj(g, w, res, gam, bet, alpha):
    n, kdim = g.shape
    m = w.shape[1]
    tm = _pick(n, 512, 2 * V7X_SUBLANES)
    tk = _pick(kdim, 512, V7X_LANES)
    tn = _pick(m // 2, 512, V7X_LANES)
    nk = kdim // tk
    return pl.pallas_call(
        functools.partial(_res_ln_kernel, nk=nk, tn=tn, alpha=alpha),
        grid=(n // tm, nk),
        in_specs=[
            pl.BlockSpec((tm, tk), lambda i, k: (i, k)),
            pl.BlockSpec((tk, m), lambda i, k: (k, 0)),
            pl.BlockSpec((tm, m), lambda i, k: (i, 0)),
            pl.BlockSpec((1, m), lambda i, k: (0, 0)),
            pl.BlockSpec((1, m), lambda i, k: (0, 0)),
        ],
        out_specs=[pl.BlockSpec((tm, m), lambda i, k: (i, 0)), pl.BlockSpec((tm, m // 2), lambda i, k: (i, 0))],
        out_shape=[jax.ShapeDtypeStruct((n, m), F32), jax.ShapeDtypeStruct((n, m // 2), U32)],
        compiler_params=_cparams(("parallel", "arbitrary")),
        name="res_ln_proj",
    )(g, w, res, gam, bet)


def _conv_kernel(h_ref, dw_ref, dwb_ref, gam_ref, bet_ref, o_ref, ext_ref, cv_ref, *, width, ts, ncb, rc):
    s = pl.program_id(1)

    @pl.when(s == 0)
    def _():
        ext_ref[:, 0:CONV_HALO, :] = jnp.zeros((ncb, CONV_HALO, V7X_LANES), F32)

    @pl.when(s > 0)
    def _():
        ext_ref[:, 0:CONV_HALO, :] = ext_ref[:, ts:ts + CONV_HALO, :]

    for cb in range(ncb):
        ext_ref[cb, CONV_HALO:CONV_HALO + ts, :] = h_ref[:, cb * V7X_LANES:(cb + 1) * V7X_LANES]

    base = CONV_HALO - (width - 1)

    def col_body(cb, carry):
        def row_body(rb, carry2):
            r0 = pl.multiple_of(rb * rc, rc)
            acc = jnp.zeros((rc, V7X_LANES), F32)
            for k in range(width):
                acc = acc + ext_ref[cb, pl.ds(r0 + base + k, rc), :] * dw_ref[cb, k:k + 1, :]
            cv_ref[cb, pl.ds(r0, rc), :] = acc + dwb_ref[cb]
            return carry2

        return lax.fori_loop(0, ts // rc, row_body, carry)

    lax.fori_loop(0, ncb, col_body, 0)

    d_model = ncb * V7X_LANES
    ssum = cv_ref[0]
    for cb in range(1, ncb):
        ssum = ssum + cv_ref[cb]
    mu = ssum.sum(axis=-1, keepdims=True) * (1.0 / d_model)
    vsum = jnp.zeros((ts, V7X_LANES), F32)
    for cb in range(ncb):
        c = cv_ref[cb] - mu
        vsum = vsum + c * c
    r = lax.rsqrt(vsum.sum(axis=-1, keepdims=True) * (1.0 / d_model) + LN_EPS)
    for cb in range(ncb):
        y = (cv_ref[cb] - mu) * r * gam_ref[cb] + bet_ref[cb]
        o_ref[:, cb * V7X_LANES:(cb + 1) * V7X_LANES] = (y * _sigmoid(y)).astype(BF16)


def _conv_ln_silu(h, dw, dwb, gam, bet, batch, seq):
    n, d = h.shape
    width = dw.shape[0]
    assert width - 1 <= CONV_HALO
    ncb = d // V7X_LANES
    ts = _pick(seq, 256, CONV_HALO)
    nsb = seq // ts
    rc = _pick(ts, 128, V7X_SUBLANES)
    dw3 = jnp.pad(dw, ((0, CONV_HALO - width), (0, 0))).reshape(CONV_HALO, ncb, V7X_LANES).transpose(1, 0, 2)
    col3 = lambda v: v.reshape(ncb, 1, V7X_LANES)
    return pl.pallas_call(
        functools.partial(_conv_kernel, width=width, ts=ts, ncb=ncb, rc=rc),
        grid=(batch, nsb),
        in_specs=[
            pl.BlockSpec((ts, d), lambda b, s: (b * nsb + s, 0)),
            pl.BlockSpec((ncb, CONV_HALO, V7X_LANES), lambda b, s: (0, 0, 0)),
            pl.BlockSpec((ncb, 1, V7X_LANES), lambda b, s: (0, 0, 0)),
            pl.BlockSpec((ncb, 1, V7X_LANES), lambda b, s: (0, 0, 0)),
            pl.BlockSpec((ncb, 1, V7X_LANES), lambda b, s: (0, 0, 0)),
        ],
        out_specs=pl.BlockSpec((ts, d), lambda b, s: (b * nsb + s, 0)),
        out_shape=jax.ShapeDtypeStruct((n, d), BF16),
        scratch_shapes=[
            pltpu.VMEM((ncb, ts + CONV_HALO, V7X_LANES), F32),
            pltpu.VMEM((ncb, ts, V7X_LANES), F32),
        ],
        compiler_params=_cparams(("arbitrary", "arbitrary")),
        name="conv_ln_silu",
    )(h, dw3, col3(dwb), col3(gam), col3(bet))


def _sgu_kernel(u_ref, v_ref, ws_ref, bs_ref, o_ref, *, heads, chunk, hd, nck):
    row = lax.broadcasted_iota(I32, (chunk, chunk), 0)
    col = lax.broadcasted_iota(I32, (chunk, chunk), 1)
    causal = row >= col
    for h in range(heads):
        wm = jnp.where(causal, ws_ref[h], 0.0).astype(BF16)
        bias = bs_ref[h]
        for c in range(nck):
            rows = slice(c * chunk, (c + 1) * chunk)
            vb = v_ref[rows, h * hd:(h + 1) * hd].astype(BF16)
            vs = jnp.dot(wm, vb, preferred_element_type=F32)
            for q in range(hd // V7X_LANES):
                cols = slice(h * hd + q * V7X_LANES, h * hd + (q + 1) * V7X_LANES)
                gate = vs[:, q * V7X_LANES:(q + 1) * V7X_LANES] + bias
                o_ref[rows, cols] = (u_ref[rows, cols] * gate).astype(BF16)


def _sgu_gate(u, v, w_s, b_s):
    n, m = u.shape
    heads, chunk, _ = w_s.shape
    hd = m // heads
    assert hd % V7X_LANES == 0
    tc = _pick(n, 256, chunk)
    nck = tc // chunk
    bsb = jnp.broadcast_to(b_s[:, :, None], (heads, chunk, V7X_LANES))
    return pl.pallas_call(
        functools.partial(_sgu_kernel, heads=heads, chunk=chunk, hd=hd, nck=nck),
        grid=(n // tc,),
        in_specs=[
            pl.BlockSpec((tc, m), lambda i: (i, 0)),
            pl.BlockSpec((tc, m), lambda i: (i, 0)),
            pl.BlockSpec((heads, chunk, chunk), lambda i: (0, 0, 0)),
            pl.BlockSpec((heads, chunk, V7X_LANES), lambda i: (0, 0, 0)),
        ],
        out_specs=pl.BlockSpec((tc, m), lambda i: (i, 0)),
        out_shape=jax.ShapeDtypeStruct((n, m), BF16),
        compiler_params=_cparams(("parallel",)),
        name="sgu_gate",
    )(u, v, w_s, bsb)


def _router_kernel(x_ref, wrh_ref, wrl_ref, rb_ref, ltri_ref, wsgu_ref, wsd_ref,
                   mask_ref, slot_ref, wslot_ref, sh_ref, *, n_exp, fpad):
    x = x_ref[...]
    xh = x.astype(BF16)
    xl = (x - xh.astype(F32)).astype(BF16)
    tm = x.shape[0]

    lg = (jnp.dot(xh, wrh_ref[...], preferred_element_type=F32)
          + jnp.dot(xl, wrh_ref[...], preferred_element_type=F32)
          + jnp.dot(xh, wrl_ref[...], preferred_element_type=F32))
    lgt = lg.T[:n_exp]
    s = _sigmoid(lgt)
    sel = s + rb_ref[...][:n_exp, 0:1]

    gsz = n_exp // N_GROUPS
    neg = jnp.float32(-jnp.inf)
    sub = lax.broadcasted_iota(I32, (gsz, tm), 0)
    gscore = []
    for g in range(N_GROUPS):
        blk = sel[g * gsz:(g + 1) * gsz]
        m1 = blk.max(axis=0, keepdims=True)
        first = jnp.where(blk == m1, sub, gsz).min(axis=0, keepdims=True)
        m2 = jnp.where(sub == first, neg, blk).max(axis=0, keepdims=True)
        gscore.append(m1 + m2)
    masked = []
    for g in range(N_GROUPS):
        rank = jnp.zeros((1, tm), I32)
        for o in range(N_GROUPS):
            if o == g:
                continue
            beats = (gscore[o] > gscore[g]) | ((gscore[o] == gscore[g]) & (o < g))
            rank = rank + beats.astype(I32)
        keep = rank < TOPK_GROUPS
        masked.append(jnp.where(keep, sel[g * gsz:(g + 1) * gsz], neg))
    mk = jnp.concatenate(masked, axis=0)

    eidx = lax.broadcasted_iota(I32, (n_exp, tm), 0)
    rank = jnp.zeros((n_exp, tm), I32)
    for o in range(n_exp):
        row = mk[o:o + 1]
        beats = (row > mk) | ((row == mk) & (eidx > o))
        rank = rank + beats.astype(I32)
    chosen = rank < TOP_K
    w = jnp.where(chosen, s, 0.0)
    denom = w.sum(axis=0, keepdims=True)
    comb = w / denom * ROUTED_SCALE

    chosen_f = jnp.where(chosen, 1.0, 0.0)
    slot = jnp.dot(ltri_ref[...], chosen_f.astype(BF16), preferred_element_type=F32).astype(I32)
    mask_ref[...] = chosen.astype(I32)
    slot_ref[...] = slot
    for k in range(TOP_K):
        wslot_ref[k:k + 1, :] = jnp.where(chosen & (slot == k), comb, 0.0).sum(axis=0, keepdims=True)

    hs = jnp.dot(xh, wsgu_ref[...], preferred_element_type=F32)
    g = hs[:, :fpad]
    act = (g * _sigmoid(g)) * hs[:, fpad:]
    sh_ref[...] = jnp.dot(act.astype(BF16), wsd_ref[...], preferred_element_type=F32)


def _router_shared(x, wrh, wrl, rb, wsgu, wsd, n_exp):
    n, d = x.shape
    fpad = wsd.shape[0]
    tm = _pick(n, 256, V7X_LANES)
    ltri = (lax.broadcasted_iota(I32, (n_exp, n_exp), 0) > lax.broadcasted_iota(I32, (n_exp, n_exp), 1)).astype(BF16)
    full = lambda a: pl.BlockSpec(a.shape, lambda i: (0,) * a.ndim)
    return pl.pallas_call(
        functools.partial(_router_kernel, n_exp=n_exp, fpad=fpad),
        grid=(n // tm,),
        in_specs=[pl.BlockSpec((tm, d), lambda i: (i, 0)), full(wrh), full(wrl), full(rb), full(ltri),
                  full(wsgu), full(wsd)],
        out_specs=[
            pl.BlockSpec((n_exp, tm), lambda i: (0, i)),
            pl.BlockSpec((n_exp, tm), lambda i: (0, i)),
            pl.BlockSpec((TOP_K, tm), lambda i: (0, i)),
            pl.BlockSpec((tm, d), lambda i: (i, 0)),
        ],
        out_shape=[
            jax.ShapeDtypeStruct((n_exp, n), I32),
            jax.ShapeDtypeStruct((n_exp, n), I32),
            jax.ShapeDtypeStruct((TOP_K, n), F32),
            jax.ShapeDtypeStruct((n, d), F32),
        ],
        compiler_params=_cparams(("parallel",)),
        name="router_shared",
    )(x, wrh, wrl, rb, ltri, wsgu, wsd)


def _experts_kernel(te_ref, tok0_ref, tokn_ref, dstp_ref, dstc_ref, x_hbm, wg_ref, wu_ref, wd_ref, y_hbm,
                    xbuf, ybuf, wgu_s, wd_s, gsem, ssem, *, tmx, ff, fpad, kc, n_steps, n_real):
    i = pl.program_id(0)
    dh = xbuf.shape[2]
    nt_dims = (((1,), (1,)), ((), ()))

    def gather_row(tok_ref, sl, r, pri):
        pltpu.make_async_copy(x_hbm.at[pl.ds(tok_ref[0, 0, r], 1)], xbuf.at[sl, pl.ds(r, 1)],
                              gsem.at[sl]).start(priority=pri)

    def scatter_row(dst_ref, sl, r, pri):
        pltpu.make_async_copy(ybuf.at[sl, pl.ds(r, 1)], y_hbm.at[pl.ds(dst_ref[0, 0, r], 1)],
                              ssem.at[sl]).start(priority=pri)

    def wait_gather(sl):
        pltpu.make_async_copy(x_hbm.at[pl.ds(0, tmx)], xbuf.at[sl], gsem.at[sl]).wait()

    def wait_scatter(sl):
        pltpu.make_async_copy(ybuf.at[sl], y_hbm.at[pl.ds(0, tmx)], ssem.at[sl]).wait()

    def rows_loop(fn):
        def body(b, c):
            base = pl.multiple_of(b * V7X_SUBLANES, V7X_SUBLANES)
            for u in range(V7X_SUBLANES):
                fn(base + u, u % 2)
            return c
        lax.fori_loop(0, tmx // V7X_SUBLANES, body, 0)

    @pl.when(i == 0)
    def _():
        wgu_s[...] = jnp.zeros(wgu_s.shape, BF16)
        wd_s[...] = jnp.zeros(wd_s.shape, BF16)
        ybuf[1] = jnp.zeros((tmx, dh), U32)
        fills = [pltpu.make_async_copy(ybuf.at[1], y_hbm.at[pl.ds(n_real + b * tmx, tmx)], ssem.at[1])
                 for b in range(2)]
        for f in fills:
            f.start()
        for f in fills:
            f.wait()
        rows_loop(lambda r, pri: gather_row(tok0_ref, 0, r, pri))

    @pl.when((i == 0) | (te_ref[i] != te_ref[jnp.maximum(i - 1, 0)]))
    def _():
        wgu_s[0:ff, :] = wg_ref[0, 0].astype(BF16)
        wgu_s[fpad:fpad + ff, :] = wu_ref[0, 0].astype(BF16)
        wd_s[0:ff, :] = wd_ref[0, 0].astype(BF16)

    def step(slot):
        other = 1 - slot

        @pl.when(i > 0)
        def _():
            wait_scatter(slot)

        wait_gather(slot)
        nkc = dh // kc
        per = tmx // nkc
        h = None
        for c in range(nkc):
            lo, hi = _unpack_bf16_pair(xbuf[slot, :, c * kc:(c + 1) * kc])
            part = (lax.dot_general(lo.astype(BF16), wgu_s[:, c * kc:(c + 1) * kc], nt_dims,
                                    preferred_element_type=F32)
                    + lax.dot_general(hi.astype(BF16), wgu_s[:, dh + c * kc:dh + (c + 1) * kc], nt_dims,
                                      preferred_element_type=F32))
            h = part if h is None else h + part
            for r in range(c * per, (c + 1) * per):
                gather_row(tokn_ref, other, r, r % 2)
                scatter_row(dstp_ref, other, r, r % 2)
        g = h[:, :fpad]
        act = ((g * _sigmoid(g)) * h[:, fpad:]).astype(BF16)
        for c in range(nkc):
            y_lo = jnp.dot(act, wd_s[:, c * kc:(c + 1) * kc], preferred_element_type=F32)
            y_hi = jnp.dot(act, wd_s[:, dh + c * kc:dh + (c + 1) * kc], preferred_element_type=F32)
            ybuf[slot, :, c * kc:(c + 1) * kc] = _pack_bf16_pair(y_lo, y_hi)

        @pl.when(i == n_steps - 1)
        def _():
            rows_loop(lambda r, pri: scatter_row(dstc_ref, slot, r, pri))
            wait_scatter(other)
            wait_scatter(slot)
            wait_gather(other)

    for parity in range(2):
        pl.when(lax.rem(i, 2) == parity)(functools.partial(step, parity))


def _routed_experts(xp, tile_expert, tok, dst_ext, w_gate_t, w_up_t, w_down, layer, n_rows_out, tmx):
    n, dh = xp.shape
    d = 2 * dh
    n_steps = tok.shape[0]
    ff = w_down.shape[2]
    fpad = -(-ff // V7X_MXU_DIM) * V7X_MXU_DIM
    kc = _pick(dh, V7X_MXU_DIM, V7X_LANES)
    smem = functools.partial(pl.BlockSpec, (1, 1, tmx), memory_space=pltpu.SMEM)
    wspec = pl.BlockSpec((1, 1, ff, d), lambda i, te: (layer, te[i], 0, 0))
    grid_spec = pltpu.PrefetchScalarGridSpec(
        num_scalar_prefetch=1,
        grid=(n_steps,),
        in_specs=[
            smem(lambda i, te: (0, 0, 0)),
            smem(lambda i, te: (jnp.minimum(i + 1, n_steps - 1), 0, 0)),
            smem(lambda i, te: (i, 0, 0)),
            smem(lambda i, te: (i + 1, 0, 0)),
            pl.BlockSpec(memory_space=pl.ANY),
            wspec, wspec, wspec,
        ],
        out_specs=pl.BlockSpec(memory_space=pl.ANY),
        scratch_shapes=[
            pltpu.VMEM((2, tmx, dh), U32),
            pltpu.VMEM((2, tmx, dh), U32),
            pltpu.VMEM((2 * fpad, d), BF16),
            pltpu.VMEM((fpad, d), BF16),
            pltpu.SemaphoreType.DMA((2,)),
            pltpu.SemaphoreType.DMA((2,)),
        ],
    )
    return pl.pallas_call(
        functools.partial(_experts_kernel, tmx=tmx, ff=ff, fpad=fpad, kc=kc, n_steps=n_steps,
                          n_real=n_rows_out - 2 * tmx),
        grid_spec=grid_spec,
        out_shape=jax.ShapeDtypeStruct((n_rows_out, dh), U32),
        compiler_params=_cparams(("arbitrary",)),
        name="routed_experts",
    )(tile_expert, tok, tok, dst_ext, dst_ext, xp, w_gate_t, w_up_t, w_down)


def _dispatch_plan(mask, slot, n, tmx, n_steps):
    pairs = n * TOP_K
    flat = jnp.nonzero(mask.reshape(-1), size=pairs, fill_value=0)[0].astype(I32)
    e_sorted = flat // n
    t_sorted = flat - e_sorted * n
    k_sorted = slot.reshape(-1)[flat]
    counts = mask.sum(axis=1).astype(I32)
    start = jnp.cumsum(counts) - counts
    tiles_per = (counts + tmx - 1) // tmx
    tile_end = jnp.cumsum(tiles_per)
    tile_start = tile_end - tiles_per
    n_tiles = tile_end[-1]
    tile_ids = jnp.minimum(jnp.arange(n_steps, dtype=I32), n_tiles - 1)
    tile_e = (tile_ids[:, None] >= tile_end[None, :]).sum(axis=1).astype(I32)
    real = jnp.arange(n_steps, dtype=I32) < n_tiles
    lane = jnp.arange(tmx, dtype=I32)[None, :]
    q = (tile_ids - tile_start[tile_e])[:, None] * tmx + lane
    valid = (q < counts[tile_e][:, None]) & real[:, None]
    r = jnp.clip(start[tile_e][:, None] + q, 0, pairs - 1)
    tok = jnp.where(valid, t_sorted[r], 0)
    bank = (jnp.arange(-1, n_steps, dtype=I32) % 2)[:, None]
    spare = pairs + bank * tmx + lane
    dst = jnp.where(valid, k_sorted[r] * n + t_sorted[r], spare[1:])
    dst_ext = jnp.concatenate([spare[:1], dst], axis=0)
    return tile_e, tok.reshape(n_steps, 1, tmx), dst_ext.reshape(n_steps + 1, 1, tmx)


def _combine_kernel(*refs, alpha):
    x_ref, sh_ref, w_ref = refs[0], refs[1], refs[2]
    y_refs = refs[3:3 + TOP_K]
    gam_ref, bet_ref, o_ref = refs[3 + TOP_K:]
    d = x_ref.shape[1]
    half = d // 2
    lo = alpha * x_ref[:, :half] + sh_ref[:, :half]
    hi = alpha * x_ref[:, half:] + sh_ref[:, half:]
    w = w_ref[...]
    for k in range(TOP_K):
        y_lo, y_hi = _unpack_bf16_pair(y_refs[k][...])
        lo = lo + w[:, k:k + 1] * y_lo
        hi = hi + w[:, k:k + 1] * y_hi
    mu = (lo.sum(axis=-1, keepdims=True) + hi.sum(axis=-1, keepdims=True)) * (1.0 / d)
    lo = lo - mu
    hi = hi - mu
    var = ((lo * lo).sum(axis=-1, keepdims=True) + (hi * hi).sum(axis=-1, keepdims=True)) * (1.0 / d)
    r = lax.rsqrt(var + LN_EPS)
    o_ref[:, :half] = lo * r * gam_ref[:, :half] + bet_ref[:, :half]
    o_ref[:, half:] = hi * r * gam_ref[:, half:] + bet_ref[:, half:]


def _moe_combine(x, shared, wtok, y, gam, bet, alpha):
    n, d = x.shape
    tm = _pick(n, 128, V7X_SUBLANES)
    nb = n // tm
    row = pl.BlockSpec((tm, d), lambda i: (i, 0))
    y_specs = [pl.BlockSpec((tm, d // 2), functools.partial(lambda i, k: (k * nb + i, 0), k=k)) for k in range(TOP_K)]
    vec = pl.BlockSpec((1, d), lambda i: (0, 0))
    return pl.pallas_call(
        functools.partial(_combine_kernel, alpha=alpha),
        grid=(nb,),
        in_specs=[row, row, pl.BlockSpec((tm, TOP_K), lambda i: (i, 0))] + y_specs + [vec, vec],
        out_specs=row,
        out_shape=jax.ShapeDtypeStruct((n, d), F32),
        compiler_params=_cparams(("parallel",)),
        name="moe_combine",
    )(x, shared, wtok, *([y] * TOP_K), gam, bet)


def _ple_kernel(x_ref, p_ref, wdn_ref, wup_ref, wpe_ref, gam_ref, bet_ref, o_ref, *, alpha):
    x = x_ref[...]
    t = jnp.dot(x.astype(BF16), wdn_ref[...], preferred_element_type=F32)
    gate = _sigmoid(jnp.dot(t.astype(BF16), wup_ref[...], preferred_element_type=F32))
    pe = jnp.dot(p_ref[...].astype(BF16), wpe_ref[...], preferred_element_type=F32)
    acc = alpha * x + gate * pe
    mu = acc.mean(axis=-1, keepdims=True)
    c = acc - mu
    var = (c * c).mean(axis=-1, keepdims=True)
    o_ref[...] = c * lax.rsqrt(var + LN_EPS) * gam_ref[...] + bet_ref[...]


def _ple_ln(x, p, wdn, wup, wpe, gam, bet, alpha):
    n, d = x.shape
    pd = p.shape[1]
    tm = _pick(n, 256, V7X_SUBLANES)
    full = lambda a: pl.BlockSpec(a.shape, lambda i: (0,) * a.ndim)
    return pl.pallas_call(
        functools.partial(_ple_kernel, alpha=alpha),
        grid=(n // tm,),
        in_specs=[pl.BlockSpec((tm, d), lambda i: (i, 0)), pl.BlockSpec((tm, pd), lambda i: (i, 0)),
                  full(wdn), full(wup), full(wpe), full(gam), full(bet)],
        out_specs=pl.BlockSpec((tm, d), lambda i: (i, 0)),
        out_shape=jax.ShapeDtypeStruct((n, d), F32),
        compiler_params=_cparams(("parallel",)),
        name="ple_ln",
    )(x, p, wdn, wup, wpe, gam, bet)


def _pad_cols(w, to):
    return jnp.pad(w, [(0, 0)] * (w.ndim - 1) + [(0, to - w.shape[-1])])


def _moe_layer(x, xp, layer, w_router, router_bias, w_gate_t, w_up_t, w_down, ws_gate, ws_up, ws_down, gam, bet,
               alpha):
    n, d = x.shape
    n_exp = w_down.shape[1]
    sfpad = -(-ws_gate.shape[1] // V7X_MXU_DIM) * V7X_MXU_DIM

    wr_t = _pad_cols(w_router.T, V7X_LANES)
    wrh = wr_t.astype(BF16)
    wrl = (wr_t - wrh.astype(F32)).astype(BF16)
    rb = jnp.broadcast_to(_pad_cols(router_bias[None, :], V7X_LANES).T, (V7X_LANES, V7X_LANES))
    wsgu = jnp.concatenate([_pad_cols(ws_gate, sfpad), _pad_cols(ws_up, sfpad)], axis=1).astype(BF16)
    wsd = jnp.pad(ws_down, ((0, sfpad - ws_down.shape[0]), (0, 0))).astype(BF16)

    mask, slot, wslot, shared = _router_shared(x, wrh, wrl, rb, wsgu, wsd, n_exp)

    tmx = _pick(n * TOP_K, 256, V7X_SUBLANES)
    n_steps = (n * TOP_K + n_exp * (tmx - 1)) // tmx + 1
    tile_e, tok, dst_ext = _dispatch_plan(mask, slot, n, tmx, n_steps)
    y = _routed_experts(xp, tile_e, tok, dst_ext, w_gate_t, w_up_t, w_down, layer, n * TOP_K + 2 * tmx, tmx)
    return _moe_combine(x, shared, wslot.T, y, gam, bet, alpha)


def kernel(x, p, ln_g, ln_b, conv_w_in, conv_b_in, conv_dw, conv_dw_b, conv_ln_g, conv_ln_b, conv_w_out, sgu_w_in, sgu_b_in, sgu_lnv_g, sgu_lnv_b, sgu_w_s, sgu_b_s, sgu_w_out, moe_w_router, moe_router_bias, moe_w_gate, moe_w_up, moe_w_down, moe_ws_gate, moe_ws_up, moe_ws_down, ple_w_gate_down, ple_w_gate_up, ple_w_proj):
    batch, seq, d = x.shape
    depth = ln_g.shape[0]
    n = batch * seq
    alpha = (2.0 * depth) ** 0.25
    vec = lambda v: v.reshape(1, -1)

    xs = x.reshape(n, d)
    w_gate_t = jnp.swapaxes(moe_w_gate, 2, 3)
    w_up_t = jnp.swapaxes(moe_w_up, 2, 3)
    for i in range(depth):
        j = i // 2
        if i % 2 == 0:
            h = _glu_proj(xs, conv_w_in[j].astype(BF16), vec(conv_b_in[j]))
            g = _conv_ln_silu(h, conv_dw[j], conv_dw_b[j], conv_ln_g[j], conv_ln_b[j], batch, seq)
            w_out = conv_w_out[j]
        else:
            w_in = sgu_w_in[j].astype(BF16)
            inner = sgu_lnv_g.shape[1]
            u = _gelu_proj(xs, w_in, vec(sgu_b_in[j]), inner)
            v = _gelu_ln_proj(xs, w_in, vec(sgu_b_in[j]), vec(sgu_lnv_g[j]), vec(sgu_lnv_b[j]), inner)
            g = _sgu_gate(u, v, sgu_w_s[j], sgu_b_s[j])
            w_out = sgu_w_out[j]
        xs, xp = _res_ln_proj(g, w_out.astype(BF16), xs, vec(ln_g[i, 0]), vec(ln_b[i, 0]), alpha)
        xs = _moe_layer(xs, xp, i, moe_w_router[i], moe_router_bias[i], w_gate_t, w_up_t, moe_w_down,
                        moe_ws_gate[i], moe_ws_up[i], moe_ws_down[i], vec(ln_g[i, 1]), vec(ln_b[i, 1]), alpha)
        xs = _ple_ln(xs, p[i].reshape(n, -1), ple_w_gate_down[i].astype(BF16), ple_w_gate_up[i].astype(BF16),
                     ple_w_proj[i].astype(BF16), vec(ln_g[i, 2]), vec(ln_b[i, 2]), alpha)
    return xs.reshape(batch, seq, d)
```

```python
import functools

import jax
import jax.numpy as jnp
from jax import lax
from jax.experimental import pallas as pl
from jax.experimental.pallas import tpu as pltpu

F32 = jnp.float32
BF16 = jnp.bfloat16
I32 = jnp.int32
U32 = jnp.uint32

TOP_K = 8
N_GROUPS = 8
TOPK_GROUPS = 4
ROUTED_SCALE = 2.5
LN_EPS = 1e-5

V7X_LANES = 128
V7X_SUBLANES = 8
V7X_MXU_DIM = 256
V7X_VMEM_BYTES = 64 * 1024 * 1024
VMEM_LIMIT = V7X_VMEM_BYTES - 4 * 1024 * 1024

CONV_HALO = 32


def _pick(n, pref, align):
    best = None
    for d in range(align, min(n, pref) + 1, align):
        if n % d == 0:
            best = d
    return n if best is None else best


def _cparams(sem):
    return pltpu.CompilerParams(dimension_semantics=sem, vmem_limit_bytes=VMEM_LIMIT)


def _sigmoid(x):
    return jax.nn.sigmoid(x)


def _pack_bf16_pair(lo, hi):
    lo_bits = lax.bitcast_convert_type(lo.astype(BF16).astype(F32), U32)
    hi_bits = lax.bitcast_convert_type(hi.astype(BF16).astype(F32), U32)
    return (lo_bits >> 16) | (hi_bits & jnp.uint32(0xFFFF0000))


def _unpack_bf16_pair(p):
    lo = lax.bitcast_convert_type(p << 16, F32)
    hi = lax.bitcast_convert_type(p & jnp.uint32(0xFFFF0000), F32)
    return lo, hi


def _glu_kernel(x_ref, wa_ref, wg_ref, ba_ref, bg_ref, o_ref, xb_ref):
    @pl.when(pl.program_id(1) == 0)
    def _():
        xb_ref[...] = x_ref[...].astype(BF16)

    xb = xb_ref[...]
    a = jnp.dot(xb, wa_ref[...], preferred_element_type=F32) + ba_ref[...]
    g = jnp.dot(xb, wg_ref[...], preferred_element_type=F32) + bg_ref[...]
    o_ref[...] = a * _sigmoid(g)


def _glu_proj(x, w, b):
    n, k = x.shape
    m = w.shape[1] // 2
    tm = _pick(n, 512, V7X_SUBLANES)
    tn = _pick(m, 512, V7X_LANES)
    nj = m // tn
    return pl.pallas_call(
        _glu_kernel,
        grid=(n // tm, nj),
        in_specs=[
            pl.BlockSpec((tm, k), lambda i, j: (i, 0)),
            pl.BlockSpec((k, tn), lambda i, j: (0, j)),
            pl.BlockSpec((k, tn), lambda i, j: (0, j + nj)),
            pl.BlockSpec((1, tn), lambda i, j: (0, j)),
            pl.BlockSpec((1, tn), lambda i, j: (0, j + nj)),
        ],
        out_specs=pl.BlockSpec((tm, tn), lambda i, j: (i, j)),
        out_shape=jax.ShapeDtypeStruct((n, m), F32),
        scratch_shapes=[pltpu.VMEM((tm, k), BF16)],
        compiler_params=_cparams(("parallel", "arbitrary")),
        name="glu_proj",
    )(x, w, w, b, b)


def _gelu(x):
    return 0.5 * x * (1.0 + lax.erf(x * (0.5 ** 0.5)))


def _gelu_kernel(x_ref, w_ref, b_ref, o_ref, xb_ref):
    @pl.when(pl.program_id(1) == 0)
    def _():
        xb_ref[...] = x_ref[...].astype(BF16)

    o_ref[...] = _gelu(jnp.dot(xb_ref[...], w_ref[...], preferred_element_type=F32) + b_ref[...])


def _gelu_proj(x, w, b, m):
    n, k = x.shape
    tm = _pick(n, 512, V7X_SUBLANES)
    tn = _pick(m, 512, V7X_LANES)
    return pl.pallas_call(
        _gelu_kernel,
        grid=(n // tm, m // tn),
        in_specs=[
            pl.BlockSpec((tm, k), lambda i, j: (i, 0)),
            pl.BlockSpec((k, tn), lambda i, j: (0, j)),
            pl.BlockSpec((1, tn), lambda i, j: (0, j)),
        ],
        out_specs=pl.BlockSpec((tm, tn), lambda i, j: (i, j)),
        out_shape=jax.ShapeDtypeStruct((n, m), F32),
        scratch_shapes=[pltpu.VMEM((tm, k), BF16)],
        compiler_params=_cparams(("parallel", "arbitrary")),
        name="gelu_proj",
    )(x, w, b)


def _accumulate_matmul(lhs, w_ref, o_ref, k, tn):
    nch = o_ref.shape[1] // tn

    def chunk(c):
        return jnp.dot(lhs, w_ref[:, c * tn:(c + 1) * tn], preferred_element_type=F32)

    @pl.when(k == 0)
    def _():
        for c in range(nch):
            o_ref[:, c * tn:(c + 1) * tn] = chunk(c)

    @pl.when(k > 0)
    def _():
        for c in range(nch):
            o_ref[:, c * tn:(c + 1) * tn] += chunk(c)


def _ln_inplace(o_ref, pre_fn, gam_ref, bet_ref, tn):
    m = o_ref.shape[1]
    nch = m // tn
    s = None
    for c in range(nch):
        p = pre_fn(c)
        o_ref[:, c * tn:(c + 1) * tn] = p
        ps = p.sum(axis=-1, keepdims=True)
        s = ps if s is None else s + ps
    mu = s * (1.0 / m)
    v = None
    for c in range(nch):
        cen = o_ref[:, c * tn:(c + 1) * tn] - mu
        pv = (cen * cen).sum(axis=-1, keepdims=True)
        v = pv if v is None else v + pv
    r = lax.rsqrt(v * (1.0 / m) + LN_EPS)
    for c in range(nch):
        cols = slice(c * tn, (c + 1) * tn)
        o_ref[:, cols] = (o_ref[:, cols] - mu) * r * gam_ref[:, cols] + bet_ref[:, cols]


def _gelu_ln_kernel(x_ref, w_ref, b_ref, gam_ref, bet_ref, o_ref, *, nk, tn):
    k = pl.program_id(1)
    _accumulate_matmul(x_ref[...].astype(BF16), w_ref, o_ref, k, tn)

    @pl.when(k == nk - 1)
    def _():
        _ln_inplace(o_ref, lambda c: _gelu(o_ref[:, c * tn:(c + 1) * tn] + b_ref[:, c * tn:(c + 1) * tn]),
                    gam_ref, bet_ref, tn)


def _gelu_ln_proj(x, w, b, gam, bet, col0):
    n, kdim = x.shape
    m = gam.shape[1]
    assert col0 % m == 0
    tm = _pick(n, 512, V7X_SUBLANES)
    tk = _pick(kdim, 512, V7X_LANES)
    tn = _pick(m, 512, V7X_LANES)
    nk = kdim // tk
    jb = col0 // m
    return pl.pallas_call(
        functools.partial(_gelu_ln_kernel, nk=nk, tn=tn),
        grid=(n // tm, nk),
        in_specs=[
            pl.BlockSpec((tm, tk), lambda i, k: (i, k)),
            pl.BlockSpec((tk, m), lambda i, k: (k, jb)),
            pl.BlockSpec((1, m), lambda i, k: (0, jb)),
            pl.BlockSpec((1, m), lambda i, k: (0, 0)),
            pl.BlockSpec((1, m), lambda i, k: (0, 0)),
        ],
        out_specs=pl.BlockSpec((tm, m), lambda i, k: (i, 0)),
        out_shape=jax.ShapeDtypeStruct((n, m), F32),
        compiler_params=_cparams(("parallel", "arbitrary")),
        name="gelu_ln_proj",
    )(x, w, b, gam, bet)


def _res_ln_kernel(g_ref, w_ref, res_ref, gam_ref, bet_ref, o_ref, op_ref, *, nk, tn, alpha):
    k = pl.program_id(1)
    _accumulate_matmul(g_ref[...], w_ref, o_ref, k, tn)

    @pl.when(k == nk - 1)
    def _():
        _ln_inplace(o_ref, lambda c: alpha * res_ref[:, c * tn:(c + 1) * tn] + o_ref[:, c * tn:(c + 1) * tn],
                    gam_ref, bet_ref, tn)
        half = o_ref.shape[1] // 2
        for c in range(half // tn):
            cols = slice(c * tn, (c + 1) * tn)
            op_ref[:, cols] = _pack_bf16_pair(o_ref[:, cols], o_ref[:, half + c * tn:half + (c + 1) * tn])


def _res_ln_proj(g, w, res, gam, bet, alpha):
    n, kdim = g.shape
    m = w.shape[1]
    tm = _pick(n, 512, 2 * V7X_SUBLANES)
    tk = _pick(kdim, 512, V7X_LANES)
    tn = _pick(m // 2, 512, V7X_LANES)
    nk = kdim // tk
    return pl.pallas_call(
        functools.partial(_res_ln_kernel, nk=nk, tn=tn, alpha=alpha),
        grid=(n // tm, nk),
        in_specs=[
            pl.BlockSpec((tm, tk), lambda i, k: (i, k)),
            pl.BlockSpec((tk, m), lambda i, k: (k, 0)),
            pl.BlockSpec((tm, m), lambda i, k: (i, 0)),
            pl.BlockSpec((1, m), lambda i, k: (0, 0)),
            pl.BlockSpec((1, m), lambda i, k: (0, 0)),
        ],
        out_specs=[pl.BlockSpec((tm, m), lambda i, k: (i, 0)), pl.BlockSpec((tm, m // 2), lambda i, k: (i, 0))],
        out_shape=[jax.ShapeDtypeStruct((n, m), F32), jax.ShapeDtypeStruct((n, m // 2), U32)],
        compiler_params=_cparams(("parallel", "arbitrary")),
        name="res_ln_proj",
    )(g, w, res, gam, bet)


def _conv_kernel(h_ref, dw_ref, dwb_ref, gam_ref, bet_ref, o_ref, ext_ref, cv_ref, *, width, ts, ncb, rc):
    s = pl.program_id(1)

    @pl.when(s == 0)
    def _():
        ext_ref[:, 0:CONV_HALO, :] = jnp.zeros((ncb, CONV_HALO, V7X_LANES), F32)

    @pl.when(s > 0)
    def _():
        ext_ref[:, 0:CONV_HALO, :] = ext_ref[:, ts:ts + CONV_HALO, :]

    for cb in range(ncb):
        ext_ref[cb, CONV_HALO:CONV_HALO + ts, :] = h_ref[:, cb * V7X_LANES:(cb + 1) * V7X_LANES]

    base = CONV_HALO - (width - 1)

    def col_body(cb, carry):
        def row_body(rb, carry2):
            r0 = pl.multiple_of(rb * rc, rc)
            acc = jnp.zeros((rc, V7X_LANES), F32)
            for k in range(width):
                acc = acc + ext_ref[cb, pl.ds(r0 + base + k, rc), :] * dw_ref[cb, k:k + 1, :]
            cv_ref[cb, pl.ds(r0, rc), :] = acc + dwb_ref[cb]
            return carry2

        return lax.fori_loop(0, ts // rc, row_body, carry)

    lax.fori_loop(0, ncb, col_body, 0)

    d_model = ncb * V7X_LANES
    ssum = cv_ref[0]
    for cb in range(1, ncb):
        ssum = ssum + cv_ref[cb]
    mu = ssum.sum(axis=-1, keepdims=True) * (1.0 / d_model)
    vsum = jnp.zeros((ts, V7X_LANES), F32)
    for cb in range(ncb):
        c = cv_ref[cb] - mu
        vsum = vsum + c * c
    r = lax.rsqrt(vsum.sum(axis=-1, keepdims=True) * (1.0 / d_model) + LN_EPS)
    for cb in range(ncb):
        y = (cv_ref[cb] - mu) * r * gam_ref[cb] + bet_ref[cb]
        o_ref[:, cb * V7X_LANES:(cb + 1) * V7X_LANES] = (y * _sigmoid(y)).astype(BF16)


def _conv_ln_silu(h, dw, dwb, gam, bet, batch, seq):
    n, d = h.shape
    width = dw.shape[0]
    assert width - 1 <= CONV_HALO
    ncb = d // V7X_LANES
    ts = _pick(seq, 256, CONV_HALO)
    nsb = seq // ts
    rc = _pick(ts, 128, V7X_SUBLANES)
    dw3 = jnp.pad(dw, ((0, CONV_HALO - width), (0, 0))).reshape(CONV_HALO, ncb, V7X_LANES).transpose(1, 0, 2)
    col3 = lambda v: v.reshape(ncb, 1, V7X_LANES)
    return pl.pallas_call(
        functools.partial(_conv_kernel, width=width, ts=ts, ncb=ncb, rc=rc),
        grid=(batch, nsb),
        in_specs=[
            pl.BlockSpec((ts, d), lambda b, s: (b * nsb + s, 0)),
            pl.BlockSpec((ncb, CONV_HALO, V7X_LANES), lambda b, s: (0, 0, 0)),
            pl.BlockSpec((ncb, 1, V7X_LANES), lambda b, s: (0, 0, 0)),
            pl.BlockSpec((ncb, 1, V7X_LANES), lambda b, s: (0, 0, 0)),
            pl.BlockSpec((ncb, 1, V7X_LANES), lambda b, s: (0, 0, 0)),
        ],
        out_specs=pl.BlockSpec((ts, d), lambda b, s: (b * nsb + s, 0)),
        out_shape=jax.ShapeDtypeStruct((n, d), BF16),
        scratch_shapes=[
            pltpu.VMEM((ncb, ts + CONV_HALO, V7X_LANES), F32),
            pltpu.VMEM((ncb, ts, V7X_LANES), F32),
        ],
        compiler_params=_cparams(("arbitrary", "arbitrary")),
        name="conv_ln_silu",
    )(h, dw3, col3(dwb), col3(gam), col3(bet))


def _sgu_kernel(u_ref, v_ref, ws_ref, bs_ref, o_ref, *, heads, chunk, hd, nck):
    row = lax.broadcasted_iota(I32, (chunk, chunk), 0)
    col = lax.broadcasted_iota(I32, (chunk, chunk), 1)
    causal = row >= col
    for h in range(heads):
        wm = jnp.where(causal, ws_ref[h], 0.0).astype(BF16)
        bias = bs_ref[h]
        for c in range(nck):
            rows = slice(c * chunk, (c + 1) * chunk)
            vb = v_ref[rows, h * hd:(h + 1) * hd].astype(BF16)
            vs = jnp.dot(wm, vb, preferred_element_type=F32)
            for q in range(hd // V7X_LANES):
                cols = slice(h * hd + q * V7X_LANES, h * hd + (q + 1) * V7X_LANES)
                gate = vs[:, q * V7X_LANES:(q + 1) * V7X_LANES] + bias
                o_ref[rows, cols] = (u_ref[rows, cols] * gate).astype(BF16)


def _sgu_gate(u, v, w_s, b_s):
    n, m = u.shape
    heads, chunk, _ = w_s.shape
    hd = m // heads
    assert hd % V7X_LANES == 0
    tc = _pick(n, 256, chunk)
    nck = tc // chunk
    bsb = jnp.broadcast_to(b_s[:, :, None], (heads, chunk, V7X_LANES))
    return pl.pallas_call(
        functools.partial(_sgu_kernel, heads=heads, chunk=chunk, hd=hd, nck=nck),
        grid=(n // tc,),
        in_specs=[
            pl.BlockSpec((tc, m), lambda i: (i, 0)),
            pl.BlockSpec((tc, m), lambda i: (i, 0)),
            pl.BlockSpec((heads, chunk, chunk), lambda i: (0, 0, 0)),
            pl.BlockSpec((heads, chunk, V7X_LANES), lambda i: (0, 0, 0)),
        ],
        out_specs=pl.BlockSpec((tc, m), lambda i: (i, 0)),
        out_shape=jax.ShapeDtypeStruct((n, m), BF16),
        compiler_params=_cparams(("parallel",)),
        name="sgu_gate",
    )(u, v, w_s, bsb)


def _router_kernel(x_ref, wrh_ref, wrl_ref, rb_ref, ltri_ref, wsgu_ref, wsd_ref,
                   mask_ref, slot_ref, wslot_ref, sh_ref, *, n_exp, fpad):
    x = x_ref[...]
    xh = x.astype(BF16)
    xl = (x - xh.astype(F32)).astype(BF16)
    tm = x.shape[0]

    lg = (jnp.dot(xh, wrh_ref[...], preferred_element_type=F32)
          + jnp.dot(xl, wrh_ref[...], preferred_element_type=F32)
          + jnp.dot(xh, wrl_ref[...], preferred_element_type=F32))
    lgt = lg.T[:n_exp]
    s = _sigmoid(lgt)
    sel = s + rb_ref[...][:n_exp, 0:1]

    gsz = n_exp // N_GROUPS
    neg = jnp.float32(-jnp.inf)
    sub = lax.broadcasted_iota(I32, (gsz, tm), 0)
    gscore = []
    for g in range(N_GROUPS):
        blk = sel[g * gsz:(g + 1) * gsz]
        m1 = blk.max(axis=0, keepdims=True)
        first = jnp.where(blk == m1, sub, gsz).min(axis=0, keepdims=True)
        m2 = jnp.where(sub == first, neg, blk).max(axis=0, keepdims=True)
        gscore.append(m1 + m2)
    masked = []
    for g in range(N_GROUPS):
        rank = jnp.zeros((1, tm), I32)
        for o in range(N_GROUPS):
            if o == g:
                continue
            beats = (gscore[o] > gscore[g]) | ((gscore[o] == gscore[g]) & (o < g))
            rank = rank + beats.astype(I32)
        keep = rank < TOPK_GROUPS
        masked.append(jnp.where(keep, sel[g * gsz:(g + 1) * gsz], neg))
    mk = jnp.concatenate(masked, axis=0)

    eidx = lax.broadcasted_iota(I32, (n_exp, tm), 0)
    rank = jnp.zeros((n_exp, tm), I32)
    for o in range(n_exp):
        row = mk[o:o + 1]
        beats = (row > mk) | ((row == mk) & (eidx > o))
        rank = rank + beats.astype(I32)
    chosen = rank < TOP_K
    w = jnp.where(chosen, s, 0.0)
    denom = w.sum(axis=0, keepdims=True)
    comb = w / denom * ROUTED_SCALE

    chosen_f = jnp.where(chosen, 1.0, 0.0)
    slot = jnp.dot(ltri_ref[...], chosen_f.astype(BF16), preferred_element_type=F32).astype(I32)
    mask_ref[...] = chosen.astype(I32)
    slot_ref[...] = slot
    for k in range(TOP_K):
        wslot_ref[k:k + 1, :] = jnp.where(chosen & (slot == k), comb, 0.0).sum(axis=0, keepdims=True)

    hs = jnp.dot(xh, wsgu_ref[...], preferred_element_type=F32)
    g = hs[:, :fpad]
    act = (g * _sigmoid(g)) * hs[:, fpad:]
    sh_ref[...] = jnp.dot(act.astype(BF16), wsd_ref[...], preferred_element_type=F32)


def _router_shared(x, wrh, wrl, rb, wsgu, wsd, n_exp):
    n, d = x.shape
    fpad = wsd.shape[0]
    tm = _pick(n, 256, V7X_LANES)
    ltri = (lax.broadcasted_iota(I32, (n_exp, n_exp), 0) > lax.broadcasted_iota(I32, (n_exp, n_exp), 1)).astype(BF16)
    full = lambda a: pl.BlockSpec(a.shape, lambda i: (0,) * a.ndim)
    return pl.pallas_call(
        functools.partial(_router_kernel, n_exp=n_exp, fpad=fpad),
        grid=(n // tm,),
        in_specs=[pl.BlockSpec((tm, d), lambda i: (i, 0)), full(wrh), full(wrl), full(rb), full(ltri),
                  full(wsgu), full(wsd)],
        out_specs=[
            pl.BlockSpec((n_exp, tm), lambda i: (0, i)),
            pl.BlockSpec((n_exp, tm), lambda i: (0, i)),
            pl.BlockSpec((TOP_K, tm), lambda i: (0, i)),
            pl.BlockSpec((tm, d), lambda i: (i, 0)),
        ],
        out_shape=[
            jax.ShapeDtypeStruct((n_exp, n), I32),
            jax.ShapeDtypeStruct((n_exp, n), I32),
            jax.ShapeDtypeStruct((TOP_K, n), F32),
            jax.ShapeDtypeStruct((n, d), F32),
        ],
        compiler_params=_cparams(("parallel",)),
        name="router_shared",
    )(x, wrh, wrl, rb, ltri, wsgu, wsd)


def _experts_kernel(te_ref, tok0_ref, tokn_ref, dstp_ref, dstc_ref, x_hbm, wg_ref, wu_ref, wd_ref, y_hbm,
                    xbuf, ybuf, wgu_s, wd_s, gsem, ssem, *, tmx, ff, fpad, kc, n_steps, n_real):
    i = pl.program_id(0)
    dh = xbuf.shape[2]
    nt_dims = (((1,), (1,)), ((), ()))

    def gather_row(tok_ref, sl, r, pri):
        pltpu.make_async_copy(x_hbm.at[pl.ds(tok_ref[0, 0, r], 1)], xbuf.at[sl, pl.ds(r, 1)],
                              gsem.at[sl]).start(priority=pri)

    def scatter_row(dst_ref, sl, r, pri):
        pltpu.make_async_copy(ybuf.at[sl, pl.ds(r, 1)], y_hbm.at[pl.ds(dst_ref[0, 0, r], 1)],
                              ssem.at[sl]).start(priority=pri)

    def wait_gather(sl):
        pltpu.make_async_copy(x_hbm.at[pl.ds(0, tmx)], xbuf.at[sl], gsem.at[sl]).wait()

    def wait_scatter(sl):
        pltpu.make_async_copy(ybuf.at[sl], y_hbm.at[pl.ds(0, tmx)], ssem.at[sl]).wait()

    def rows_loop(fn):
        def body(b, c):
            base = pl.multiple_of(b * V7X_SUBLANES, V7X_SUBLANES)
            for u in range(V7X_SUBLANES):
                fn(base + u, u % 2)
            return c
        lax.fori_loop(0, tmx // V7X_SUBLANES, body, 0)

    @pl.when(i == 0)
    def _():
        wgu_s[...] = jnp.zeros(wgu_s.shape, BF16)
        wd_s[...] = jnp.zeros(wd_s.shape, BF16)
        ybuf[1] = jnp.zeros((tmx, dh), U32)
        fills = [pltpu.make_async_copy(ybuf.at[1], y_hbm.at[pl.ds(n_real + b * tmx, tmx)], ssem.at[1])
                 for b in range(2)]
        for f in fills:
            f.start()
        for f in fills:
            f.wait()
        rows_loop(lambda r, pri: gather_row(tok0_ref, 0, r, pri))

    @pl.when((i == 0) | (te_ref[i] != te_ref[jnp.maximum(i - 1, 0)]))
    def _():
        wgu_s[0:ff, :] = wg_ref[0, 0].astype(BF16)
        wgu_s[fpad:fpad + ff, :] = wu_ref[0, 0].astype(BF16)
        wd_s[0:ff, :] = wd_ref[0, 0].astype(BF16)

    def step(slot):
        other = 1 - slot
        wait_gather(slot)
        nkc = dh // kc
        g_chunks = list(range(max(nkc // 2, 1)))
        s_chunks = list(range(nkc // 2, nkc)) if nkc > 1 else [0]
        h = None
        for c in range(nkc):
            lo, hi = _unpack_bf16_pair(xbuf[slot, :, c * kc:(c + 1) * kc])
            part = (lax.dot_general(lo.astype(BF16), wgu_s[:, c * kc:(c + 1) * kc], nt_dims,
                                    preferred_element_type=F32)
                    + lax.dot_general(hi.astype(BF16), wgu_s[:, dh + c * kc:dh + (c + 1) * kc], nt_dims,
                                      preferred_element_type=F32))
            h = part if h is None else h + part
            if c in g_chunks:
                per = tmx // len(g_chunks)
                for r in range(g_chunks.index(c) * per, (g_chunks.index(c) + 1) * per):
                    gather_row(tokn_ref, other, r, r % 2)
            if c in s_chunks:
                per = tmx // len(s_chunks)
                for r in range(s_chunks.index(c) * per, (s_chunks.index(c) + 1) * per):
                    scatter_row(dstp_ref, other, r, r % 2)
        g = h[:, :fpad]
        act = ((g * _sigmoid(g)) * h[:, fpad:]).astype(BF16)

        @pl.when(i > 0)
        def _():
            wait_scatter(slot)

        for c in range(nkc):
            y_lo = jnp.dot(act, wd_s[:, c * kc:(c + 1) * kc], preferred_element_type=F32)
            y_hi = jnp.dot(act, wd_s[:, dh + c * kc:dh + (c + 1) * kc], preferred_element_type=F32)
            ybuf[slot, :, c * kc:(c + 1) * kc] = _pack_bf16_pair(y_lo, y_hi)

        @pl.when(i == n_steps - 1)
        def _():
            rows_loop(lambda r, pri: scatter_row(dstc_ref, slot, r, pri))
            wait_scatter(other)
            wait_scatter(slot)
            wait_gather(other)

    for parity in range(2):
        pl.when(lax.rem(i, 2) == parity)(functools.partial(step, parity))


def _routed_experts(xp, tile_expert, tok, dst_ext, w_gate_t, w_up_t, w_down, layer, n_rows_out, tmx):
    n, dh = xp.shape
    d = 2 * dh
    n_steps = tok.shape[0]
    ff = w_down.shape[2]
    fpad = -(-ff // V7X_MXU_DIM) * V7X_MXU_DIM
    kc = _pick(dh, V7X_MXU_DIM, V7X_LANES)
    smem = functools.partial(pl.BlockSpec, (1, 1, tmx), memory_space=pltpu.SMEM)
    wspec = pl.BlockSpec((1, 1, ff, d), lambda i, te: (layer, te[i], 0, 0))
    grid_spec = pltpu.PrefetchScalarGridSpec(
        num_scalar_prefetch=1,
        grid=(n_steps,),
        in_specs=[
            smem(lambda i, te: (0, 0, 0)),
            smem(lambda i, te: (jnp.minimum(i + 1, n_steps - 1), 0, 0)),
            smem(lambda i, te: (i, 0, 0)),
            smem(lambda i, te: (i + 1, 0, 0)),
            pl.BlockSpec(memory_space=pl.ANY),
            wspec, wspec, wspec,
        ],
        out_specs=pl.BlockSpec(memory_space=pl.ANY),
        scratch_shapes=[
            pltpu.VMEM((2, tmx, dh), U32),
            pltpu.VMEM((2, tmx, dh), U32),
            pltpu.VMEM((2 * fpad, d), BF16),
            pltpu.VMEM((fpad, d), BF16),
            pltpu.SemaphoreType.DMA((2,)),
            pltpu.SemaphoreType.DMA((2,)),
        ],
    )
    return pl.pallas_call(
        functools.partial(_experts_kernel, tmx=tmx, ff=ff, fpad=fpad, kc=kc, n_steps=n_steps,
                          n_real=n_rows_out - 2 * tmx),
        grid_spec=grid_spec,
        out_shape=jax.ShapeDtypeStruct((n_rows_out, dh), U32),
        compiler_params=_cparams(("arbitrary",)),
        name="routed_experts",
    )(tile_expert, tok, tok, dst_ext, dst_ext, xp, w_gate_t, w_up_t, w_down)


def _dispatch_plan(mask, slot, n, tmx, n_steps):
    pairs = n * TOP_K
    flat = jnp.nonzero(mask.reshape(-1), size=pairs, fill_value=0)[0].astype(I32)
    e_sorted = flat // n
    t_sorted = flat - e_sorted * n
    k_sorted = slot.reshape(-1)[flat]
    counts = mask.sum(axis=1).astype(I32)
    start = jnp.cumsum(counts) - counts
    tiles_per = (counts + tmx - 1) // tmx
    tile_end = jnp.cumsum(tiles_per)
    tile_start = tile_end - tiles_per
    n_tiles = tile_end[-1]
    tile_ids = jnp.minimum(jnp.arange(n_steps, dtype=I32), n_tiles - 1)
    tile_e = (tile_ids[:, None] >= tile_end[None, :]).sum(axis=1).astype(I32)
    real = jnp.arange(n_steps, dtype=I32) < n_tiles
    lane = jnp.arange(tmx, dtype=I32)[None, :]
    q = (tile_ids - tile_start[tile_e])[:, None] * tmx + lane
    valid = (q < counts[tile_e][:, None]) & real[:, None]
    r = jnp.clip(start[tile_e][:, None] + q, 0, pairs - 1)
    tok = jnp.where(valid, t_sorted[r], 0)
    bank = (jnp.arange(-1, n_steps, dtype=I32) % 2)[:, None]
    spare = pairs + bank * tmx + lane
    dst = jnp.where(valid, k_sorted[r] * n + t_sorted[r], spare[1:])
    dst_ext = jnp.concatenate([spare[:1], dst], axis=0)
    return tile_e, tok.reshape(n_steps, 1, tmx), dst_ext.reshape(n_steps + 1, 1, tmx)


def _combine_kernel(*refs, alpha):
    x_ref, sh_ref, w_ref = refs[0], refs[1], refs[2]
    y_refs = refs[3:3 + TOP_K]
    gam_ref, bet_ref, o_ref = refs[3 + TOP_K:]
    d = x_ref.shape[1]
    half = d // 2
    lo = alpha * x_ref[:, :half] + sh_ref[:, :half]
    hi = alpha * x_ref[:, half:] + sh_ref[:, half:]
    w = w_ref[...]
    for k in range(TOP_K):
        y_lo, y_hi = _unpack_bf16_pair(y_refs[k][...])
        lo = lo + w[:, k:k + 1] * y_lo
        hi = hi + w[:, k:k + 1] * y_hi
    mu = (lo.sum(axis=-1, keepdims=True) + hi.sum(axis=-1, keepdims=True)) * (1.0 / d)
    lo = lo - mu
    hi = hi - mu
    var = ((lo * lo).sum(axis=-1, keepdims=True) + (hi * hi).sum(axis=-1, keepdims=True)) * (1.0 / d)
    r = lax.rsqrt(var + LN_EPS)
    o_ref[:, :half] = lo * r * gam_ref[:, :half] + bet_ref[:, :half]
    o_ref[:, half:] = hi * r * gam_ref[:, half:] + bet_ref[:, half:]


def _moe_combine(x, shared, wtok, y, gam, bet, alpha):
    n, d = x.shape
    tm = _pick(n, 128, V7X_SUBLANES)
    nb = n // tm
    row = pl.BlockSpec((tm, d), lambda i: (i, 0))
    y_specs = [pl.BlockSpec((tm, d // 2), functools.partial(lambda i, k: (k * nb + i, 0), k=k)) for k in range(TOP_K)]
    vec = pl.BlockSpec((1, d), lambda i: (0, 0))
    return pl.pallas_call(
        functools.partial(_combine_kernel, alpha=alpha),
        grid=(nb,),
        in_specs=[row, row, pl.BlockSpec((tm, TOP_K), lambda i: (i, 0))] + y_specs + [vec, vec],
        out_specs=row,
        out_shape=jax.ShapeDtypeStruct((n, d), F32),
        compiler_params=_cparams(("parallel",)),
        name="moe_combine",
    )(x, shared, wtok, *([y] * TOP_K), gam, bet)


def _ple_kernel(x_ref, p_ref, wdn_ref, wup_ref, wpe_ref, gam_ref, bet_ref, o_ref, *, alpha):
    x = x_ref[...]
    t = jnp.dot(x.astype(BF16), wdn_ref[...], preferred_element_type=F32)
    gate = _sigmoid(jnp.dot(t.astype(BF16), wup_ref[...], preferred_element_type=F32))
    pe = jnp.dot(p_ref[...].astype(BF16), wpe_ref[...], preferred_element_type=F32)
    acc = alpha * x + gate * pe
    mu = acc.mean(axis=-1, keepdims=True)
    c = acc - mu
    var = (c * c).mean(axis=-1, keepdims=True)
    o_ref[...] = c * lax.rsqrt(var + LN_EPS) * gam_ref[...] + bet_ref[...]


def _ple_ln(x, p, wdn, wup, wpe, gam, bet, alpha):
    n, d = x.shape
    pd = p.shape[1]
    tm = _pick(n, 256, V7X_SUBLANES)
    full = lambda a: pl.BlockSpec(a.shape, lambda i: (0,) * a.ndim)
    return pl.pallas_call(
        functools.partial(_ple_kernel, alpha=alpha),
        grid=(n // tm,),
        in_specs=[pl.BlockSpec((tm, d), lambda i: (i, 0)), pl.BlockSpec((tm, pd), lambda i: (i, 0)),
                  full(wdn), full(wup), full(wpe), full(gam), full(bet)],
        out_specs=pl.BlockSpec((tm, d), lambda i: (i, 0)),
        out_shape=jax.ShapeDtypeStruct((n, d), F32),
        compiler_params=_cparams(("parallel",)),
        name="ple_ln",
    )(x, p, wdn, wup, wpe, gam, bet)


def _pad_cols(w, to):
    return jnp.pad(w, [(0, 0)] * (w.ndim - 1) + [(0, to - w.shape[-1])])


def _moe_layer(x, xp, layer, w_router, router_bias, w_gate_t, w_up_t, w_down, ws_gate, ws_up, ws_down, gam, bet,
               alpha):
    n, d = x.shape
    n_exp = w_down.shape[1]
    sfpad = -(-ws_gate.shape[1] // V7X_MXU_DIM) * V7X_MXU_DIM

    wr_t = _pad_cols(w_router.T, V7X_LANES)
    wrh = wr_t.astype(BF16)
    wrl = (wr_t - wrh.astype(F32)).astype(BF16)
    rb = jnp.broadcast_to(_pad_cols(router_bias[None, :], V7X_LANES).T, (V7X_LANES, V7X_LANES))
    wsgu = jnp.concatenate([_pad_cols(ws_gate, sfpad), _pad_cols(ws_up, sfpad)], axis=1).astype(BF16)
    wsd = jnp.pad(ws_down, ((0, sfpad - ws_down.shape[0]), (0, 0))).astype(BF16)

    mask, slot, wslot, shared = _router_shared(x, wrh, wrl, rb, wsgu, wsd, n_exp)

    tmx = _pick(n * TOP_K, 256, V7X_SUBLANES)
    n_steps = (n * TOP_K + n_exp * (tmx - 1)) // tmx + 1
    tile_e, tok, dst_ext = _dispatch_plan(mask, slot, n, tmx, n_steps)
    y = _routed_experts(xp, tile_e, tok, dst_ext, w_gate_t, w_up_t, w_down, layer, n * TOP_K + 2 * tmx, tmx)
    return _moe_combine(x, shared, wslot.T, y, gam, bet, alpha)


def kernel(x, p, ln_g, ln_b, conv_w_in, conv_b_in, conv_dw, conv_dw_b, conv_ln_g, conv_ln_b, conv_w_out, sgu_w_in, sgu_b_in, sgu_lnv_g, sgu_lnv_b, sgu_w_s, sgu_b_s, sgu_w_out, moe_w_router, moe_router_bias, moe_w_gate, moe_w_up, moe_w_down, moe_ws_gate, moe_ws_up, moe_ws_down, ple_w_gate_down, ple_w_gate_up, ple_w_proj):
    batch, seq, d = x.shape
    depth = ln_g.shape[0]
    n = batch * seq
    alpha = (2.0 * depth) ** 0.25
    vec = lambda v: v.reshape(1, -1)

    xs = x.reshape(n, d)
    w_gate_t = jnp.swapaxes(moe_w_gate, 2, 3)
    w_up_t = jnp.swapaxes(moe_w_up, 2, 3)
    for i in range(depth):
        j = i // 2
        if i % 2 == 0:
            h = _glu_proj(xs, conv_w_in[j].astype(BF16), vec(conv_b_in[j]))
            g = _conv_ln_silu(h, conv_dw[j], conv_dw_b[j], conv_ln_g[j], conv_ln_b[j], batch, seq)
            w_out = conv_w_out[j]
        else:
            w_in = sgu_w_in[j].astype(BF16)
            inner = sgu_lnv_g.shape[1]
            u = _gelu_proj(xs, w_in, vec(sgu_b_in[j]), inner)
            v = _gelu_ln_proj(xs, w_in, vec(sgu_b_in[j]), vec(sgu_lnv_g[j]), vec(sgu_lnv_b[j]), inner)
            g = _sgu_gate(u, v, sgu_w_s[j], sgu_b_s[j])
            w_out = sgu_w_out[j]
        xs, xp = _res_ln_proj(g, w_out.astype(BF16), xs, vec(ln_g[i, 0]), vec(ln_b[i, 0]), alpha)
        xs = _moe_layer(xs, xp, i, moe_w_router[i], moe_router_bias[i], w_gate_t, w_up_t, moe_w_down,
                        moe_ws_gate[i], moe_ws_up[i], moe_ws_down[i], vec(ln_g[i, 1]), vec(ln_b[i, 1]), alpha)
        xs = _ple_ln(xs, p[i].reshape(n, -1), ple_w_gate_down[i].astype(BF16), ple_w_gate_up[i].astype(BF16),
                     ple_w_proj[i].astype(BF16), vec(ln_g[i, 2]), vec(ln_b[i, 2]), alpha)
    return xs.reshape(batch, seq, d)
```

```python
import functools

import jax
import jax.numpy as jnp
from jax import lax
from jax.experimental import pallas as pl
from jax.experimental.pallas import tpu as pltpu

F32 = jnp.float32
BF16 = jnp.bfloat16
I32 = jnp.int32
U32 = jnp.uint32

TOP_K = 8
N_GROUPS = 8
TOPK_GROUPS = 4
ROUTED_SCALE = 2.5
LN_EPS = 1e-5

V7X_LANES = 128
V7X_SUBLANES = 8
V7X_MXU_DIM = 256
V7X_VMEM_BYTES = 64 * 1024 * 1024
VMEM_LIMIT = V7X_VMEM_BYTES - 4 * 1024 * 1024

CONV_HALO = 32


def _pick(n, pref, align):
    best = None
    for d in range(align, min(n, pref) + 1, align):
        if n % d == 0:
            best = d
    return n if best is None else best


def _cparams(sem):
    return pltpu.CompilerParams(dimension_semantics=sem, vmem_limit_bytes=VMEM_LIMIT)


def _sigmoid(x):
    return jax.nn.sigmoid(x)


def _pack_bf16_pair(lo, hi):
    lo_bits = lax.bitcast_convert_type(lo.astype(BF16).astype(F32), U32)
    hi_bits = lax.bitcast_convert_type(hi.astype(BF16).astype(F32), U32)
    return (lo_bits >> 16) | (hi_bits & jnp.uint32(0xFFFF0000))


def _unpack_bf16_pair(p):
    lo = lax.bitcast_convert_type(p << 16, F32)
    hi = lax.bitcast_convert_type(p & jnp.uint32(0xFFFF0000), F32)
    return lo, hi


def _glu_kernel(x_ref, wa_ref, wg_ref, ba_ref, bg_ref, o_ref, xb_ref):
    @pl.when(pl.program_id(1) == 0)
    def _():
        xb_ref[...] = x_ref[...].astype(BF16)

    xb = xb_ref[...]
    a = jnp.dot(xb, wa_ref[...], preferred_element_type=F32) + ba_ref[...]
    g = jnp.dot(xb, wg_ref[...], preferred_element_type=F32) + bg_ref[...]
    o_ref[...] = a * _sigmoid(g)


def _glu_proj(x, w, b):
    n, k = x.shape
    m = w.shape[1] // 2
    tm = _pick(n, 512, V7X_SUBLANES)
    tn = _pick(m, 512, V7X_LANES)
    nj = m // tn
    return pl.pallas_call(
        _glu_kernel,
        grid=(n // tm, nj),
        in_specs=[
            pl.BlockSpec((tm, k), lambda i, j: (i, 0)),
            pl.BlockSpec((k, tn), lambda i, j: (0, j)),
            pl.BlockSpec((k, tn), lambda i, j: (0, j + nj)),
            pl.BlockSpec((1, tn), lambda i, j: (0, j)),
            pl.BlockSpec((1, tn), lambda i, j: (0, j + nj)),
        ],
        out_specs=pl.BlockSpec((tm, tn), lambda i, j: (i, j)),
        out_shape=jax.ShapeDtypeStruct((n, m), F32),
        scratch_shapes=[pltpu.VMEM((tm, k), BF16)],
        compiler_params=_cparams(("parallel", "arbitrary")),
        name="glu_proj",
    )(x, w, w, b, b)


def _gelu(x):
    return 0.5 * x * (1.0 + lax.erf(x * (0.5 ** 0.5)))


def _gelu_kernel(x_ref, w_ref, b_ref, o_ref, xb_ref):
    @pl.when(pl.program_id(1) == 0)
    def _():
        xb_ref[...] = x_ref[...].astype(BF16)

    o_ref[...] = _gelu(jnp.dot(xb_ref[...], w_ref[...], preferred_element_type=F32) + b_ref[...])


def _gelu_proj(x, w, b, m):
    n, k = x.shape
    tm = _pick(n, 512, V7X_SUBLANES)
    tn = _pick(m, 512, V7X_LANES)
    return pl.pallas_call(
        _gelu_kernel,
        grid=(n // tm, m // tn),
        in_specs=[
            pl.BlockSpec((tm, k), lambda i, j: (i, 0)),
            pl.BlockSpec((k, tn), lambda i, j: (0, j)),
            pl.BlockSpec((1, tn), lambda i, j: (0, j)),
        ],
        out_specs=pl.BlockSpec((tm, tn), lambda i, j: (i, j)),
        out_shape=jax.ShapeDtypeStruct((n, m), F32),
        scratch_shapes=[pltpu.VMEM((tm, k), BF16)],
        compiler_params=_cparams(("parallel", "arbitrary")),
        name="gelu_proj",
    )(x, w, b)


def _accumulate_matmul(lhs, w_ref, o_ref, k, tn):
    nch = o_ref.shape[1] // tn

    def chunk(c):
        return jnp.dot(lhs, w_ref[:, c * tn:(c + 1) * tn], preferred_element_type=F32)

    @pl.when(k == 0)
    def _():
        for c in range(nch):
            o_ref[:, c * tn:(c + 1) * tn] = chunk(c)

    @pl.when(k > 0)
    def _():
        for c in range(nch):
            o_ref[:, c * tn:(c + 1) * tn] += chunk(c)


def _ln_inplace(o_ref, pre_fn, gam_ref, bet_ref, tn):
    m = o_ref.shape[1]
    nch = m // tn
    s = None
    for c in range(nch):
        p = pre_fn(c)
        o_ref[:, c * tn:(c + 1) * tn] = p
        ps = p.sum(axis=-1, keepdims=True)
        s = ps if s is None else s + ps
    mu = s * (1.0 / m)
    v = None
    for c in range(nch):
        cen = o_ref[:, c * tn:(c + 1) * tn] - mu
        pv = (cen * cen).sum(axis=-1, keepdims=True)
        v = pv if v is None else v + pv
    r = lax.rsqrt(v * (1.0 / m) + LN_EPS)
    for c in range(nch):
        cols = slice(c * tn, (c + 1) * tn)
        o_ref[:, cols] = (o_ref[:, cols] - mu) * r * gam_ref[:, cols] + bet_ref[:, cols]


def _gelu_ln_kernel(x_ref, w_ref, b_ref, gam_ref, bet_ref, o_ref, *, nk, tn):
    k = pl.program_id(1)
    _accumulate_matmul(x_ref[...].astype(BF16), w_ref, o_ref, k, tn)

    @pl.when(k == nk - 1)
    def _():
        _ln_inplace(o_ref, lambda c: _gelu(o_ref[:, c * tn:(c + 1) * tn] + b_ref[:, c * tn:(c + 1) * tn]),
                    gam_ref, bet_ref, tn)


def _gelu_ln_proj(x, w, b, gam, bet, col0):
    n, kdim = x.shape
    m = gam.shape[1]
    assert col0 % m == 0
    tm = _pick(n, 512, V7X_SUBLANES)
    tk = _pick(kdim, 512, V7X_LANES)
    tn = _pick(m, 512, V7X_LANES)
    nk = kdim // tk
    jb = col0 // m
    return pl.pallas_call(
        functools.partial(_gelu_ln_kernel, nk=nk, tn=tn),
        grid=(n // tm, nk),
        in_specs=[
            pl.BlockSpec((tm, tk), lambda i, k: (i, k)),
            pl.BlockSpec((tk, m), lambda i, k: (k, jb)),
            pl.BlockSpec((1, m), lambda i, k: (0, jb)),
            pl.BlockSpec((1, m), lambda i, k: (0, 0)),
            pl.BlockSpec((1, m), lambda i, k: (0, 0)),
        ],
        out_specs=pl.BlockSpec((tm, m), lambda i, k: (i, 0)),
        out_shape=jax.ShapeDtypeStruct((n, m), F32),
        compiler_params=_cparams(("parallel", "arbitrary")),
        name="gelu_ln_proj",
    )(x, w, b, gam, bet)


def _res_ln_kernel(g_ref, w_ref, res_ref, gam_ref, bet_ref, o_ref, op_ref, *, nk, tn, alpha):
    k = pl.program_id(1)
    _accumulate_matmul(g_ref[...], w_ref, o_ref, k, tn)

    @pl.when(k == nk - 1)
    def _():
        _ln_inplace(o_ref, lambda c: alpha * res_ref[:, c * tn:(c + 1) * tn] + o_ref[:, c * tn:(c + 1) * tn],
                    gam_ref, bet_ref, tn)
        half = o_ref.shape[1] // 2
        for c in range(half // tn):
            cols = slice(c * tn, (c + 1) * tn)
            op_ref[:, cols] = _pack_bf16_pair(o_ref[:, cols], o_ref[:, half + c * tn:half + (c + 1) * tn])


def _res_ln_proj(g, w, res, gam, bet, alpha):
    n, kdim = g.shape
    m = w.shape[1]
    tm = _pick(n, 512, 2 * V7X_SUBLANES)
    tk = _pick(kdim, 512, V7X_LANES)
    tn = _pick(m // 2, 512, V7X_LANES)
    nk = kdim // tk
    return pl.pallas_call(
        functools.partial(_res_ln_kernel, nk=nk, tn=tn, alpha=alpha),
        grid=(n // tm, nk),
        in_specs=[
            pl.BlockSpec((tm, tk), lambda i, k: (i, k)),
            pl.BlockSpec((tk, m), lambda i, k: (k, 0)),
            pl.BlockSpec((tm, m), lambda i, k: (i, 0)),
            pl.BlockSpec((1, m), lambda i, k: (0, 0)),
            pl.BlockSpec((1, m), lambda i, k: (0, 0)),
        ],
        out_specs=[pl.BlockSpec((tm, m), lambda i, k: (i, 0)), pl.BlockSpec((tm, m // 2), lambda i, k: (i, 0))],
        out_shape=[jax.ShapeDtypeStruct((n, m), F32), jax.ShapeDtypeStruct((n, m // 2), U32)],
        compiler_params=_cparams(("parallel", "arbitrary")),
        name="res_ln_proj",
    )(g, w, res, gam, bet)


def _conv_kernel(h_ref, dw_ref, dwb_ref, gam_ref, bet_ref, o_ref, ext_ref, cv_ref, *, width, ts, ncb, rc):
    s = pl.program_id(1)

    @pl.when(s == 0)
    def _():
        ext_ref[:, 0:CONV_HALO, :] = jnp.zeros((ncb, CONV_HALO, V7X_LANES), F32)

    @pl.when(s > 0)
    def _():
        ext_ref[:, 0:CONV_HALO, :] = ext_ref[:, ts:ts + CONV_HALO, :]

    for cb in range(ncb):
        ext_ref[cb, CONV_HALO:CONV_HALO + ts, :] = h_ref[:, cb * V7X_LANES:(cb + 1) * V7X_LANES]

    base = CONV_HALO - (width - 1)

    def col_body(cb, carry):
        def row_body(rb, carry2):
            r0 = pl.multiple_of(rb * rc, rc)
            acc = jnp.zeros((rc, V7X_LANES), F32)
            for k in range(width):
                acc = acc + ext_ref[cb, pl.ds(r0 + base + k, rc), :] * dw_ref[cb, k:k + 1, :]
            cv_ref[cb, pl.ds(r0, rc), :] = acc + dwb_ref[cb]
            return carry2

        return lax.fori_loop(0, ts // rc, row_body, carry)

    lax.fori_loop(0, ncb, col_body, 0)

    d_model = ncb * V7X_LANES
    ssum = cv_ref[0]
    for cb in range(1, ncb):
        ssum = ssum + cv_ref[cb]
    mu = ssum.sum(axis=-1, keepdims=True) * (1.0 / d_model)
    vsum = jnp.zeros((ts, V7X_LANES), F32)
    for cb in range(ncb):
        c = cv_ref[cb] - mu
        vsum = vsum + c * c
    r = lax.rsqrt(vsum.sum(axis=-1, keepdims=True) * (1.0 / d_model) + LN_EPS)
    for cb in range(ncb):
        y = (cv_ref[cb] - mu) * r * gam_ref[cb] + bet_ref[cb]
        o_ref[:, cb * V7X_LANES:(cb + 1) * V7X_LANES] = (y * _sigmoid(y)).astype(BF16)


def _conv_ln_silu(h, dw, dwb, gam, bet, batch, seq):
    n, d = h.shape
    width = dw.shape[0]
    assert width - 1 <= CONV_HALO
    ncb = d // V7X_LANES
    ts = _pick(seq, 256, CONV_HALO)
    nsb = seq // ts
    rc = _pick(ts, 128, V7X_SUBLANES)
    dw3 = jnp.pad(dw, ((0, CONV_HALO - width), (0, 0))).reshape(CONV_HALO, ncb, V7X_LANES).transpose(1, 0, 2)
    col3 = lambda v: v.reshape(ncb, 1, V7X_LANES)
    return pl.pallas_call(
        functools.partial(_conv_kernel, width=width, ts=ts, ncb=ncb, rc=rc),
        grid=(batch, nsb),
        in_specs=[
            pl.BlockSpec((ts, d), lambda b, s: (b * nsb + s, 0)),
            pl.BlockSpec((ncb, CONV_HALO, V7X_LANES), lambda b, s: (0, 0, 0)),
            pl.BlockSpec((ncb, 1, V7X_LANES), lambda b, s: (0, 0, 0)),
            pl.BlockSpec((ncb, 1, V7X_LANES), lambda b, s: (0, 0, 0)),
            pl.BlockSpec((ncb, 1, V7X_LANES), lambda b, s: (0, 0, 0)),
        ],
        out_specs=pl.BlockSpec((ts, d), lambda b, s: (b * nsb + s, 0)),
        out_shape=jax.ShapeDtypeStruct((n, d), BF16),
        scratch_shapes=[
            pltpu.VMEM((ncb, ts + CONV_HALO, V7X_LANES), F32),
            pltpu.VMEM((ncb, ts, V7X_LANES), F32),
        ],
        compiler_params=_cparams(("arbitrary", "arbitrary")),
        name="conv_ln_silu",
    )(h, dw3, col3(dwb), col3(gam), col3(bet))


def _sgu_kernel(u_ref, v_ref, ws_ref, bs_ref, o_ref, *, heads, chunk, hd, nck):
    row = lax.broadcasted_iota(I32, (chunk, chunk), 0)
    col = lax.broadcasted_iota(I32, (chunk, chunk), 1)
    causal = row >= col
    for h in range(heads):
        wm = jnp.where(causal, ws_ref[h], 0.0).astype(BF16)
        bias = bs_ref[h]
        for c in range(nck):
            rows = slice(c * chunk, (c + 1) * chunk)
            vb = v_ref[rows, h * hd:(h + 1) * hd].astype(BF16)
            vs = jnp.dot(wm, vb, preferred_element_type=F32)
            for q in range(hd // V7X_LANES):
                cols = slice(h * hd + q * V7X_LANES, h * hd + (q + 1) * V7X_LANES)
                gate = vs[:, q * V7X_LANES:(q + 1) * V7X_LANES] + bias
                o_ref[rows, cols] = (u_ref[rows, cols] * gate).astype(BF16)


def _sgu_gate(u, v, w_s, b_s):
    n, m = u.shape
    heads, chunk, _ = w_s.shape
    hd = m // heads
    assert hd % V7X_LANES == 0
    tc = _pick(n, 256, chunk)
    nck = tc // chunk
    bsb = jnp.broadcast_to(b_s[:, :, None], (heads, chunk, V7X_LANES))
    return pl.pallas_call(
        functools.partial(_sgu_kernel, heads=heads, chunk=chunk, hd=hd, nck=nck),
        grid=(n // tc,),
        in_specs=[
            pl.BlockSpec((tc, m), lambda i: (i, 0)),
            pl.BlockSpec((tc, m), lambda i: (i, 0)),
            pl.BlockSpec((heads, chunk, chunk), lambda i: (0, 0, 0)),
            pl.BlockSpec((heads, chunk, V7X_LANES), lambda i: (0, 0, 0)),
        ],
        out_specs=pl.BlockSpec((tc, m), lambda i: (i, 0)),
        out_shape=jax.ShapeDtypeStruct((n, m), BF16),
        compiler_params=_cparams(("parallel",)),
        name="sgu_gate",
    )(u, v, w_s, bsb)


def _router_kernel(x_ref, wrh_ref, wrl_ref, rb_ref, ltri_ref, wsgu_ref, wsd_ref,
                   mask_ref, slot_ref, wslot_ref, sh_ref, *, n_exp, fpad):
    x = x_ref[...]
    xh = x.astype(BF16)
    xl = (x - xh.astype(F32)).astype(BF16)
    tm = x.shape[0]

    lg = (jnp.dot(xh, wrh_ref[...], preferred_element_type=F32)
          + jnp.dot(xl, wrh_ref[...], preferred_element_type=F32)
          + jnp.dot(xh, wrl_ref[...], preferred_element_type=F32))
    lgt = lg.T[:n_exp]
    s = _sigmoid(lgt)
    sel = s + rb_ref[...][:n_exp, 0:1]

    gsz = n_exp // N_GROUPS
    neg = jnp.float32(-jnp.inf)
    sub = lax.broadcasted_iota(I32, (gsz, tm), 0)
    gscore = []
    for g in range(N_GROUPS):
        blk = sel[g * gsz:(g + 1) * gsz]
        m1 = blk.max(axis=0, keepdims=True)
        first = jnp.where(blk == m1, sub, gsz).min(axis=0, keepdims=True)
        m2 = jnp.where(sub == first, neg, blk).max(axis=0, keepdims=True)
        gscore.append(m1 + m2)
    masked = []
    for g in range(N_GROUPS):
        rank = jnp.zeros((1, tm), I32)
        for o in range(N_GROUPS):
            if o == g:
                continue
            beats = (gscore[o] > gscore[g]) | ((gscore[o] == gscore[g]) & (o < g))
            rank = rank + beats.astype(I32)
        keep = rank < TOPK_GROUPS
        masked.append(jnp.where(keep, sel[g * gsz:(g + 1) * gsz], neg))
    mk = jnp.concatenate(masked, axis=0)

    eidx = lax.broadcasted_iota(I32, (n_exp, tm), 0)
    rank = jnp.zeros((n_exp, tm), I32)
    for o in range(n_exp):
        row = mk[o:o + 1]
        beats = (row > mk) | ((row == mk) & (eidx > o))
        rank = rank + beats.astype(I32)
    chosen = rank < TOP_K
    w = jnp.where(chosen, s, 0.0)
    denom = w.sum(axis=0, keepdims=True)
    comb = w / denom * ROUTED_SCALE

    chosen_f = jnp.where(chosen, 1.0, 0.0)
    slot = jnp.dot(ltri_ref[...], chosen_f.astype(BF16), preferred_element_type=F32).astype(I32)
    mask_ref[...] = chosen.astype(I32)
    slot_ref[...] = slot
    for k in range(TOP_K):
        wslot_ref[k:k + 1, :] = jnp.where(chosen & (slot == k), comb, 0.0).sum(axis=0, keepdims=True)

    hs = jnp.dot(xh, wsgu_ref[...], preferred_element_type=F32)
    g = hs[:, :fpad]
    act = (g * _sigmoid(g)) * hs[:, fpad:]
    sh_ref[...] = jnp.dot(act.astype(BF16), wsd_ref[...], preferred_element_type=F32)


def _router_shared(x, wrh, wrl, rb, wsgu, wsd, n_exp):
    n, d = x.shape
    fpad = wsd.shape[0]
    tm = _pick(n, 256, V7X_LANES)
    ltri = (lax.broadcasted_iota(I32, (n_exp, n_exp), 0) > lax.broadcasted_iota(I32, (n_exp, n_exp), 1)).astype(BF16)
    full = lambda a: pl.BlockSpec(a.shape, lambda i: (0,) * a.ndim)
    return pl.pallas_call(
        functools.partial(_router_kernel, n_exp=n_exp, fpad=fpad),
        grid=(n // tm,),
        in_specs=[pl.BlockSpec((tm, d), lambda i: (i, 0)), full(wrh), full(wrl), full(rb), full(ltri),
                  full(wsgu), full(wsd)],
        out_specs=[
            pl.BlockSpec((n_exp, tm), lambda i: (0, i)),
            pl.BlockSpec((n_exp, tm), lambda i: (0, i)),
            pl.BlockSpec((TOP_K, tm), lambda i: (0, i)),
            pl.BlockSpec((tm, d), lambda i: (i, 0)),
        ],
        out_shape=[
            jax.ShapeDtypeStruct((n_exp, n), I32),
            jax.ShapeDtypeStruct((n_exp, n), I32),
            jax.ShapeDtypeStruct((TOP_K, n), F32),
            jax.ShapeDtypeStruct((n, d), F32),
        ],
        compiler_params=_cparams(("parallel",)),
        name="router_shared",
    )(x, wrh, wrl, rb, ltri, wsgu, wsd)


def _experts_kernel(te_ref, tok0_ref, tokn_ref, dstp_ref, dstc_ref, x_hbm, wg_ref, wu_ref, wd_ref, y_hbm,
                    xbuf, ybuf, wgu_s, wd_s, gsem, ssem, *, tmx, ff, fpad, kc, n_steps, n_real):
    i = pl.program_id(0)
    n_tiles = te_ref[n_steps]
    dh = xbuf.shape[2]
    nt_dims = (((1,), (1,)), ((), ()))

    def gather_row(tok_ref, sl, r, pri):
        pltpu.make_async_copy(x_hbm.at[pl.ds(tok_ref[0, 0, r], 1)], xbuf.at[sl, pl.ds(r, 1)],
                              gsem.at[sl]).start(priority=pri)

    def scatter_row(dst_ref, sl, r, pri):
        pltpu.make_async_copy(ybuf.at[sl, pl.ds(r, 1)], y_hbm.at[pl.ds(dst_ref[0, 0, r], 1)],
                              ssem.at[sl]).start(priority=pri)

    def wait_gather(sl):
        pltpu.make_async_copy(x_hbm.at[pl.ds(0, tmx)], xbuf.at[sl], gsem.at[sl]).wait()

    def wait_scatter(sl):
        pltpu.make_async_copy(ybuf.at[sl], y_hbm.at[pl.ds(0, tmx)], ssem.at[sl]).wait()

    def rows_loop(fn):
        def body(b, c):
            base = pl.multiple_of(b * V7X_SUBLANES, V7X_SUBLANES)
            for u in range(V7X_SUBLANES):
                fn(base + u, u % 2)
            return c
        lax.fori_loop(0, tmx // V7X_SUBLANES, body, 0)

    @pl.when(i == 0)
    def _():
        wgu_s[...] = jnp.zeros(wgu_s.shape, BF16)
        wd_s[...] = jnp.zeros(wd_s.shape, BF16)
        ybuf[1] = jnp.zeros((tmx, dh), U32)
        fills = [pltpu.make_async_copy(ybuf.at[1], y_hbm.at[pl.ds(n_real + b * tmx, tmx)], ssem.at[1])
                 for b in range(2)]
        for f in fills:
            f.start()
        for f in fills:
            f.wait()
        rows_loop(lambda r, pri: gather_row(tok0_ref, 0, r, pri))

    @pl.when((i == 0) | (te_ref[i] != te_ref[jnp.maximum(i - 1, 0)]))
    def _():
        wgu_s[0:ff, :] = wg_ref[0, 0].astype(BF16)
        wgu_s[fpad:fpad + ff, :] = wu_ref[0, 0].astype(BF16)
        wd_s[0:ff, :] = wd_ref[0, 0].astype(BF16)

    def step(slot):
        other = 1 - slot
        wait_gather(slot)
        nkc = dh // kc
        g_chunks = list(range(max(nkc // 2, 1)))
        s_chunks = list(range(nkc // 2, nkc)) if nkc > 1 else [0]
        h = None
        for c in range(nkc):
            lo, hi = _unpack_bf16_pair(xbuf[slot, :, c * kc:(c + 1) * kc])
            part = (lax.dot_general(lo.astype(BF16), wgu_s[:, c * kc:(c + 1) * kc], nt_dims,
                                    preferred_element_type=F32)
                    + lax.dot_general(hi.astype(BF16), wgu_s[:, dh + c * kc:dh + (c + 1) * kc], nt_dims,
                                      preferred_element_type=F32))
            h = part if h is None else h + part
            if c in g_chunks:
                per = tmx // len(g_chunks)
                for r in range(g_chunks.index(c) * per, (g_chunks.index(c) + 1) * per):
                    gather_row(tokn_ref, other, r, r % 2)
            if c in s_chunks:
                per = tmx // len(s_chunks)
                for r in range(s_chunks.index(c) * per, (s_chunks.index(c) + 1) * per):
                    scatter_row(dstp_ref, other, r, r % 2)
        g = h[:, :fpad]
        act = ((g * _sigmoid(g)) * h[:, fpad:]).astype(BF16)

        @pl.when(i > 0)
        def _():
            wait_scatter(slot)

        for c in range(nkc):
            y_lo = jnp.dot(act, wd_s[:, c * kc:(c + 1) * kc], preferred_element_type=F32)
            y_hi = jnp.dot(act, wd_s[:, dh + c * kc:dh + (c + 1) * kc], preferred_element_type=F32)
            ybuf[slot, :, c * kc:(c + 1) * kc] = _pack_bf16_pair(y_lo, y_hi)

        @pl.when(i == n_tiles - 1)
        def _():
            rows_loop(lambda r, pri: scatter_row(dstc_ref, slot, r, pri))
            wait_scatter(other)
            wait_scatter(slot)
            wait_gather(other)

    for parity in range(2):
        pl.when((lax.rem(i, 2) == parity) & (i < n_tiles))(functools.partial(step, parity))


def _routed_experts(xp, tile_expert, tok, dst_ext, w_gate_t, w_up_t, w_down, layer, n_rows_out, tmx):
    n, dh = xp.shape
    d = 2 * dh
    n_steps = tok.shape[0]
    ff = w_down.shape[2]
    fpad = -(-ff // V7X_MXU_DIM) * V7X_MXU_DIM
    kc = _pick(dh, V7X_MXU_DIM, V7X_LANES)
    smem = functools.partial(pl.BlockSpec, (1, 1, tmx), memory_space=pltpu.SMEM)
    wspec = pl.BlockSpec((1, 1, ff, d), lambda i, te: (layer, te[i], 0, 0))
    grid_spec = pltpu.PrefetchScalarGridSpec(
        num_scalar_prefetch=1,
        grid=(n_steps,),
        in_specs=[
            smem(lambda i, te: (0, 0, 0)),
            smem(lambda i, te: (jnp.minimum(i + 1, n_steps - 1), 0, 0)),
            smem(lambda i, te: (i, 0, 0)),
            smem(lambda i, te: (i + 1, 0, 0)),
            pl.BlockSpec(memory_space=pl.ANY),
            wspec, wspec, wspec,
        ],
        out_specs=pl.BlockSpec(memory_space=pl.ANY),
        scratch_shapes=[
            pltpu.VMEM((2, tmx, dh), U32),
            pltpu.VMEM((2, tmx, dh), U32),
            pltpu.VMEM((2 * fpad, d), BF16),
            pltpu.VMEM((fpad, d), BF16),
            pltpu.SemaphoreType.DMA((2,)),
            pltpu.SemaphoreType.DMA((2,)),
        ],
    )
    return pl.pallas_call(
        functools.partial(_experts_kernel, tmx=tmx, ff=ff, fpad=fpad, kc=kc, n_steps=n_steps,
                          n_real=n_rows_out - 2 * tmx),
        grid_spec=grid_spec,
        out_shape=jax.ShapeDtypeStruct((n_rows_out, dh), U32),
        compiler_params=_cparams(("arbitrary",)),
        name="routed_experts",
    )(tile_expert, tok, tok, dst_ext, dst_ext, xp, w_gate_t, w_up_t, w_down)


def _dispatch_plan(mask, slot, n, tmx, n_steps):
    pairs = n * TOP_K
    flat = jnp.nonzero(mask.reshape(-1), size=pairs, fill_value=0)[0].astype(I32)
    e_sorted = flat // n
    t_sorted = flat - e_sorted * n
    k_sorted = slot.reshape(-1)[flat]
    counts = mask.sum(axis=1).astype(I32)
    start = jnp.cumsum(counts) - counts
    tiles_per = (counts + tmx - 1) // tmx
    tile_end = jnp.cumsum(tiles_per)
    tile_start = tile_end - tiles_per
    n_tiles = tile_end[-1]
    tile_ids = jnp.minimum(jnp.arange(n_steps, dtype=I32), n_tiles - 1)
    tile_e = (tile_ids[:, None] >= tile_end[None, :]).sum(axis=1).astype(I32)
    real = jnp.arange(n_steps, dtype=I32) < n_tiles
    lane = jnp.arange(tmx, dtype=I32)[None, :]
    q = (tile_ids - tile_start[tile_e])[:, None] * tmx + lane
    valid = (q < counts[tile_e][:, None]) & real[:, None]
    r = jnp.clip(start[tile_e][:, None] + q, 0, pairs - 1)
    tok = jnp.where(valid, t_sorted[r], 0)
    bank = (jnp.arange(-1, n_steps, dtype=I32) % 2)[:, None]
    spare = pairs + bank * tmx + lane
    dst = jnp.where(valid, k_sorted[r] * n + t_sorted[r], spare[1:])
    dst_ext = jnp.concatenate([spare[:1], dst], axis=0)
    tile_e = jnp.concatenate([tile_e, n_tiles.reshape(1).astype(I32)])
    return tile_e, tok.reshape(n_steps, 1, tmx), dst_ext.reshape(n_steps + 1, 1, tmx)


def _combine_kernel(*refs, alpha):
    x_ref, sh_ref, w_ref = refs[0], refs[1], refs[2]
    y_refs = refs[3:3 + TOP_K]
    gam_ref, bet_ref, o_ref = refs[3 + TOP_K:]
    d = x_ref.shape[1]
    half = d // 2
    lo = alpha * x_ref[:, :half] + sh_ref[:, :half]
    hi = alpha * x_ref[:, half:] + sh_ref[:, half:]
    w = w_ref[...]
    for k in range(TOP_K):
        y_lo, y_hi = _unpack_bf16_pair(y_refs[k][...])
        lo = lo + w[:, k:k + 1] * y_lo
        hi = hi + w[:, k:k + 1] * y_hi
    mu = (lo.sum(axis=-1, keepdims=True) + hi.sum(axis=-1, keepdims=True)) * (1.0 / d)
    lo = lo - mu
    hi = hi - mu
    var = ((lo * lo).sum(axis=-1, keepdims=True) + (hi * hi).sum(axis=-1, keepdims=True)) * (1.0 / d)
    r = lax.rsqrt(var + LN_EPS)
    o_ref[:, :half] = lo * r * gam_ref[:, :half] + bet_ref[:, :half]
    o_ref[:, half:] = hi * r * gam_ref[:, half:] + bet_ref[:, half:]


def _moe_combine(x, shared, wtok, y, gam, bet, alpha):
    n, d = x.shape
    tm = _pick(n, 128, V7X_SUBLANES)
    nb = n // tm
    row = pl.BlockSpec((tm, d), lambda i: (i, 0))
    y_specs = [pl.BlockSpec((tm, d // 2), functools.partial(lambda i, k: (k * nb + i, 0), k=k)) for k in range(TOP_K)]
    vec = pl.BlockSpec((1, d), lambda i: (0, 0))
    return pl.pallas_call(
        functools.partial(_combine_kernel, alpha=alpha),
        grid=(nb,),
        in_specs=[row, row, pl.BlockSpec((tm, TOP_K), lambda i: (i, 0))] + y_specs + [vec, vec],
        out_specs=row,
        out_shape=jax.ShapeDtypeStruct((n, d), F32),
        compiler_params=_cparams(("parallel",)),
        name="moe_combine",
    )(x, shared, wtok, *([y] * TOP_K), gam, bet)


def _ple_kernel(x_ref, p_ref, wdn_ref, wup_ref, wpe_ref, gam_ref, bet_ref, o_ref, *, alpha):
    x = x_ref[...]
    t = jnp.dot(x.astype(BF16), wdn_ref[...], preferred_element_type=F32)
    gate = _sigmoid(jnp.dot(t.astype(BF16), wup_ref[...], preferred_element_type=F32))
    pe = jnp.dot(p_ref[...].astype(BF16), wpe_ref[...], preferred_element_type=F32)
    acc = alpha * x + gate * pe
    mu = acc.mean(axis=-1, keepdims=True)
    c = acc - mu
    var = (c * c).mean(axis=-1, keepdims=True)
    o_ref[...] = c * lax.rsqrt(var + LN_EPS) * gam_ref[...] + bet_ref[...]


def _ple_ln(x, p, wdn, wup, wpe, gam, bet, alpha):
    n, d = x.shape
    pd = p.shape[1]
    tm = _pick(n, 256, V7X_SUBLANES)
    full = lambda a: pl.BlockSpec(a.shape, lambda i: (0,) * a.ndim)
    return pl.pallas_call(
        functools.partial(_ple_kernel, alpha=alpha),
        grid=(n // tm,),
        in_specs=[pl.BlockSpec((tm, d), lambda i: (i, 0)), pl.BlockSpec((tm, pd), lambda i: (i, 0)),
                  full(wdn), full(wup), full(wpe), full(gam), full(bet)],
        out_specs=pl.BlockSpec((tm, d), lambda i: (i, 0)),
        out_shape=jax.ShapeDtypeStruct((n, d), F32),
        compiler_params=_cparams(("parallel",)),
        name="ple_ln",
    )(x, p, wdn, wup, wpe, gam, bet)


def _pad_cols(w, to):
    return jnp.pad(w, [(0, 0)] * (w.ndim - 1) + [(0, to - w.shape[-1])])


def _moe_layer(x, xp, layer, w_router, router_bias, w_gate_t, w_up_t, w_down, ws_gate, ws_up, ws_down, gam, bet,
               alpha):
    n, d = x.shape
    n_exp = w_down.shape[1]
    sfpad = -(-ws_gate.shape[1] // V7X_MXU_DIM) * V7X_MXU_DIM

    wr_t = _pad_cols(w_router.T, V7X_LANES)
    wrh = wr_t.astype(BF16)
    wrl = (wr_t - wrh.astype(F32)).astype(BF16)
    rb = jnp.broadcast_to(_pad_cols(router_bias[None, :], V7X_LANES).T, (V7X_LANES, V7X_LANES))
    wsgu = jnp.concatenate([_pad_cols(ws_gate, sfpad), _pad_cols(ws_up, sfpad)], axis=1).astype(BF16)
    wsd = jnp.pad(ws_down, ((0, sfpad - ws_down.shape[0]), (0, 0))).astype(BF16)

    mask, slot, wslot, shared = _router_shared(x, wrh, wrl, rb, wsgu, wsd, n_exp)

    tmx = _pick(n * TOP_K, 256, V7X_SUBLANES)
    n_steps = (n * TOP_K + n_exp * (tmx - 1)) // tmx + 1
    tile_e, tok, dst_ext = _dispatch_plan(mask, slot, n, tmx, n_steps)
    y = _routed_experts(xp, tile_e, tok, dst_ext, w_gate_t, w_up_t, w_down, layer, n * TOP_K + 2 * tmx, tmx)
    return _moe_combine(x, shared, wslot.T, y, gam, bet, alpha)


def kernel(x, p, ln_g, ln_b, conv_w_in, conv_b_in, conv_dw, conv_dw_b, conv_ln_g, conv_ln_b, conv_w_out, sgu_w_in, sgu_b_in, sgu_lnv_g, sgu_lnv_b, sgu_w_s, sgu_b_s, sgu_w_out, moe_w_router, moe_router_bias, moe_w_gate, moe_w_up, moe_w_down, moe_ws_gate, moe_ws_up, moe_ws_down, ple_w_gate_down, ple_w_gate_up, ple_w_proj):
    batch, seq, d = x.shape
    depth = ln_g.shape[0]
    n = batch * seq
    alpha = (2.0 * depth) ** 0.25
    vec = lambda v: v.reshape(1, -1)

    xs = x.reshape(n, d)
    w_gate_t = jnp.swapaxes(moe_w_gate, 2, 3)
    w_up_t = jnp.swapaxes(moe_w_up, 2, 3)
    for i in range(depth):
        j = i // 2
        if i % 2 == 0:
            h = _glu_proj(xs, conv_w_in[j].astype(BF16), vec(conv_b_in[j]))
            g = _conv_ln_silu(h, conv_dw[j], conv_dw_b[j], conv_ln_g[j], conv_ln_b[j], batch, seq)
            w_out = conv_w_out[j]
        else:
            w_in = sgu_w_in[j].astype(BF16)
            inner = sgu_lnv_g.shape[1]
            u = _gelu_proj(xs, w_in, vec(sgu_b_in[j]), inner)
            v = _gelu_ln_proj(xs, w_in, vec(sgu_b_in[j]), vec(sgu_lnv_g[j]), vec(sgu_lnv_b[j]), inner)
            g = _sgu_gate(u, v, sgu_w_s[j], sgu_b_s[j])
            w_out = sgu_w_out[j]
        xs, xp = _res_ln_proj(g, w_out.astype(BF16), xs, vec(ln_g[i, 0]), vec(ln_b[i, 0]), alpha)
        xs = _moe_layer(xs, xp, i, moe_w_router[i], moe_router_bias[i], w_gate_t, w_up_t, moe_w_down,
                        moe_ws_gate[i], moe_ws_up[i], moe_ws_down[i], vec(ln_g[i, 1]), vec(ln_b[i, 1]), alpha)
        xs = _ple_ln(xs, p[i].reshape(n, -1), ple_w_gate_down[i].astype(BF16), ple_w_gate_up[i].astype(BF16),
                     ple_w_proj[i].astype(BF16), vec(ln_g[i, 2]), vec(ln_b[i, 2]), alpha)
    return xs.reshape(batch, seq, d)
```
